```python
import math
import jax, jax.numpy as jnp
from jax import lax
import numpy as np

D_MODEL = 1024
BATCH = 8
SEQ = 4096
DEPTH = 4

HEAD_DIM = 64
N_NSA_HEADS = 4
N_SSD_HEADS = 8
N_FOX_HEADS = 4
D_NSA = N_NSA_HEADS * HEAD_DIM
D_SSD = N_SSD_HEADS * HEAD_DIM
D_FOX = N_FOX_HEADS * HEAD_DIM
D_MIX = D_NSA + D_SSD + D_FOX
NSA_N_KV_TENSORS = 6
NSA_CMP_BLOCK = 32
NSA_CMP_STRIDE = 16
NSA_SLC_BLOCK = 64
NSA_TOP_N = 16
NSA_WINDOW = 512
NSA_Q_BLOCK = 64
SSD_GROUPS = 2
SSD_D_STATE = 64
SSD_CONV = 4
SSD_CHUNK = 128
SSD_CONV_DIM = D_SSD + 2 * SSD_GROUPS * SSD_D_STATE
FOX_Q_BLOCK = 128
D_FF = 2816
N_EXPERTS = 8
TOP_K = 2
MOE_BLOCK = 128
N_DENSE = (DEPTH + 1) // 2
N_MOE = DEPTH // 2
RMS_EPS = 1e-6
IN_SIZES = (D_NSA, NSA_N_KV_TENSORS * HEAD_DIM, 3 * N_NSA_HEADS,
            D_SSD, SSD_CONV_DIM, N_SSD_HEADS,
            D_FOX, D_FOX, D_FOX, N_FOX_HEADS)
D_IN = sum(IN_SIZES)

kernel_name = 'hybrid_nsa_ssd_fox_moe_block'


def rms_norm(x, g):
    x32 = x.astype(jnp.float32)
    y = x32 * lax.rsqrt(jnp.mean(x32 * x32, axis=-1, keepdims=True) + RMS_EPS)
    return (y * g.astype(jnp.float32)).astype(x.dtype)


def masked_softmax(s, mask):
    s = jnp.where(mask, s.astype(jnp.float32), -jnp.inf)
    m = jnp.max(s, axis=-1, keepdims=True)
    e = jnp.exp(s - jnp.where(jnp.isfinite(m), m, 0.0))
    return e / jnp.maximum(jnp.sum(e, axis=-1, keepdims=True), jnp.finfo(jnp.float32).tiny)


def swiglu(h, w1, w3, w2):
    return (jax.nn.silu(h @ w1) * (h @ w3)) @ w2


def nsa_mixer(q, kv, gates, q_norm, k_norm, pe_k, pe_v, w_ck, w_cv):
    bsz, s = q.shape[0], q.shape[1]
    scale = HEAD_DIM ** -0.5
    q = rms_norm(q, q_norm)
    n_cmp = (s - NSA_CMP_BLOCK) // NSA_CMP_STRIDE + 1
    n_slc = s // NSA_SLC_BLOCK
    top_n = min(NSA_TOP_N, n_slc)
    c_start = jnp.arange(n_cmp) * NSA_CMP_STRIDE
    cmp_idx = c_start[:, None] + jnp.arange(NSA_CMP_BLOCK)[None, :]

    def compress(t, pe, w):
        blocks = t[:, cmp_idx] + pe
        return blocks.reshape(bsz, n_cmp, NSA_CMP_BLOCK * HEAD_DIM) @ w

    kc = rms_norm(compress(kv[:, :, 0], pe_k, w_ck), k_norm)
    vc = compress(kv[:, :, 1], pe_v, w_cv)
    ks = rms_norm(kv[:, :, 2], k_norm).reshape(bsz, n_slc, NSA_SLC_BLOCK, HEAD_DIM)
    vs = kv[:, :, 3].reshape(bsz, n_slc, NSA_SLC_BLOCK, HEAD_DIM)
    kw = jnp.pad(rms_norm(kv[:, :, 4], k_norm), ((0, 0), (NSA_WINDOW, 0), (0, 0)))
    vw = jnp.pad(kv[:, :, 5], ((0, 0), (NSA_WINDOW, 0), (0, 0)))
    n_start = jnp.arange(n_slc) * NSA_SLC_BLOCK
    overlap = ((c_start[:, None] < n_start[None, :] + NSA_SLC_BLOCK)
               & (c_start[:, None] + NSA_CMP_BLOCK > n_start[None, :])).astype(jnp.float32)
    cmp_end = c_start + NSA_CMP_BLOCK - 1
    g = jax.nn.sigmoid(gates.astype(jnp.float32))
    bidx = jnp.arange(bsz)[:, None, None]
    blk_ids = jnp.arange(n_slc)

    def block(i):
        q0 = i * NSA_Q_BLOCK
        t = q0 + jnp.arange(NSA_Q_BLOCK)
        qb = lax.dynamic_slice_in_dim(q, q0, NSA_Q_BLOCK, axis=1)
        gb = lax.dynamic_slice_in_dim(g, q0, NSA_Q_BLOCK, axis=1)
        s_c = jnp.einsum('bqhd,bcd->bhqc', qb, kc) * scale
        p_c = masked_softmax(s_c, cmp_end[None, :] <= t[:, None])
        o_c = jnp.einsum('bhqc,bcd->bqhd', p_c.astype(vc.dtype), vc)
        imp = jnp.einsum('bhqc,cn->bqn', p_c, overlap)
        avail = n_start[None, :] <= t[:, None]
        forced = (blk_ids[None, :] == (t // NSA_SLC_BLOCK)[:, None]) | (blk_ids[None, :] == 0)
        score = jnp.where(avail[None], jnp.where(forced[None], jnp.inf, imp), -jnp.inf)
        _, sel = lax.top_k(score, top_n)
        kg = ks[bidx, sel].reshape(bsz, NSA_Q_BLOCK, top_n * NSA_SLC_BLOCK, HEAD_DIM)
        vg = vs[bidx, sel].reshape(bsz, NSA_Q_BLOCK, top_n * NSA_SLC_BLOCK, HEAD_DIM)
        pos = (sel[..., None] * NSA_SLC_BLOCK + jnp.arange(NSA_SLC_BLOCK)).reshape(
            bsz, NSA_Q_BLOCK, top_n * NSA_SLC_BLOCK)
        s_s = jnp.einsum('bqhd,bqkd->bhqk', qb, kg) * scale
        p_s = masked_softmax(s_s, (pos <= t[None, :, None])[:, None])
        o_s = jnp.einsum('bhqk,bqkd->bqhd', p_s.astype(vg.dtype), vg)
        kwb = lax.dynamic_slice_in_dim(kw, q0, NSA_WINDOW + NSA_Q_BLOCK, axis=1)
        vwb = lax.dynamic_slice_in_dim(vw, q0, NSA_WINDOW + NSA_Q_BLOCK, axis=1)
        s_pos = q0 - NSA_WINDOW + jnp.arange(NSA_WINDOW + NSA_Q_BLOCK)
        mask_w = ((s_pos[None, :] <= t[:, None]) & (s_pos[None, :] > t[:, None] - NSA_WINDOW)
                  & (s_pos[None, :] >= 0))
        s_w = jnp.einsum('bqhd,bkd->bhqk', qb, kwb) * scale
        p_w = masked_softmax(s_w, mask_w)
        o_w = jnp.einsum('bhqk,bkd->bqhd', p_w.astype(vwb.dtype), vwb)
        out = gb[..., 0:1] * o_c + gb[..., 1:2] * o_s + gb[..., 2:3] * o_w
        return out.astype(q.dtype)

    out = lax.map(block, jnp.arange(s // NSA_Q_BLOCK))
    return out.transpose(1, 0, 2, 3, 4).reshape(bsz, s, D_NSA)


def ssd_chunked_scan(x, dt, a, b_in, c_in):
    bsz, s, h, p = x.shape
    n = b_in.shape[-1]
    rep = h // b_in.shape[2]
    nc = s // SSD_CHUNK
    ln = SSD_CHUNK
    b_h = jnp.repeat(b_in, rep, axis=2).reshape(bsz, nc, ln, h, n)
    c_h = jnp.repeat(c_in, rep, axis=2).reshape(bsz, nc, ln, h, n)
    xdt = (x * dt[..., None]).reshape(bsz, nc, ln, h, p)
    a_cs = jnp.cumsum((dt * a).reshape(bsz, nc, ln, h).transpose(0, 3, 1, 2), axis=-1)
    causal = jnp.tril(jnp.ones((ln, ln), dtype=bool))
    decay_in = jnp.exp(jnp.where(causal, a_cs[..., :, None] - a_cs[..., None, :], -jnp.inf))
    scores = jnp.einsum('bclhn,bcshn->bhcls', c_h, b_h) * decay_in
    y_diag = jnp.einsum('bhcls,bcshp->bclhp', scores, xdt)
    decay_to_end = jnp.exp(a_cs[..., -1:] - a_cs)
    chunk_states = jnp.einsum('bclhn,bhcl,bclhp->bchpn', b_h, decay_to_end, xdt)
    chunk_decay = jnp.exp(a_cs[..., -1])

    def step(state, inp):
        st, dec = inp
        return state * dec[..., None, None] + st, state

    init = jnp.zeros((bsz, h, p, n), x.dtype)
    _, prev = lax.scan(step, init, (jnp.moveaxis(chunk_states, 1, 0), jnp.moveaxis(chunk_decay, 2, 0)))
    prev = jnp.moveaxis(prev, 0, 1)
    y_off = jnp.einsum('bclhn,bchpn,bhcl->bclhp', c_h, prev, jnp.exp(a_cs))
    return (y_diag + y_off).reshape(bsz, s, h, p)


def ssd_mixer(z, xbc, dt_raw, conv_w, conv_b, dt_bias, a_log, d_skip, norm_g):
    bsz, s = z.shape[0], z.shape[1]
    xbc = lax.conv_general_dilated(xbc, conv_w[:, None, :], window_strides=(1,),
                                   padding=((SSD_CONV - 1, 0),),
                                   dimension_numbers=('NWC', 'WIO', 'NWC'),
                                   feature_group_count=SSD_CONV_DIM) + conv_b
    xbc = jax.nn.silu(xbc).astype(jnp.float32)
    xs = xbc[..., :D_SSD].reshape(bsz, s, N_SSD_HEADS, HEAD_DIM)
    b_in = xbc[..., D_SSD:D_SSD + SSD_GROUPS * SSD_D_STATE].reshape(bsz, s, SSD_GROUPS, SSD_D_STATE)
    c_in = xbc[..., D_SSD + SSD_GROUPS * SSD_D_STATE:].reshape(bsz, s, SSD_GROUPS, SSD_D_STATE)
    dt = jax.nn.softplus(dt_raw.astype(jnp.float32) + dt_bias.astype(jnp.float32))
    a = -jnp.exp(a_log.astype(jnp.float32))
    y = ssd_chunked_scan(xs, dt, a, b_in, c_in) + xs * d_skip.astype(jnp.float32)[:, None]
    y = y.reshape(bsz, s, D_SSD) * jax.nn.silu(z.astype(jnp.float32))
    return rms_norm(y, norm_g).astype(z.dtype)


def fox_mixer(q, k, v, f_logit, q_norm, k_norm, f_bias):
    bsz, s = q.shape[0], q.shape[1]
    scale = HEAD_DIM ** -0.5
    q = rms_norm(q, q_norm)
    k = rms_norm(k, k_norm)
    log_f = jax.nn.log_sigmoid(f_logit.astype(jnp.float32) + f_bias.astype(jnp.float32))
    c = jnp.cumsum(log_f, axis=1).transpose(0, 2, 1)
    s_idx = jnp.arange(s)

    def block(i):
        q0 = i * FOX_Q_BLOCK
        t = q0 + jnp.arange(FOX_Q_BLOCK)
        qb = lax.dynamic_slice_in_dim(q, q0, FOX_Q_BLOCK, axis=1)
        cq = lax.dynamic_slice_in_dim(c, q0, FOX_Q_BLOCK, axis=2)
        logits = (jnp.einsum('bqhd,bkhd->bhqk', qb, k).astype(jnp.float32) * scale
                  + cq[..., None] - c[:, :, None, :])
        p = masked_softmax(logits, s_idx[None, :] <= t[:, None])
        return jnp.einsum('bhqk,bkhd->bqhd', p.astype(v.dtype), v)

    out = lax.map(block, jnp.arange(s // FOX_Q_BLOCK))
    return out.transpose(1, 0, 2, 3, 4).reshape(bsz, s, D_FOX)


def moe_ffn(h, w_router, w1, w3, w2):
    bsz, s, d = h.shape
    n_tok = bsz * s
    xt = h.reshape(n_tok, d)
    logits = (xt @ w_router).astype(jnp.float32)
    top_v, top_i = lax.top_k(logits, TOP_K)
    gates = jax.nn.softmax(top_v, axis=-1)
    n_rows = n_tok * TOP_K
    flat_e = top_i.reshape(-1)
    flat_tok = jnp.arange(n_rows) // TOP_K
    flat_g = gates.reshape(-1)
    order = jnp.argsort(flat_e)
    e_sorted = flat_e[order]
    tok_sorted = flat_tok[order]
    counts = jnp.bincount(flat_e, length=N_EXPERTS)
    padded = ((counts + MOE_BLOCK - 1) // MOE_BLOCK) * MOE_BLOCK
    start = jnp.cumsum(counts) - counts
    pend = jnp.cumsum(padded)
    pstart = pend - padded
    dest = pstart[e_sorted] + (jnp.arange(n_rows) - start[e_sorted])
    n_blk = -(-n_rows // MOE_BLOCK) + N_EXPERTS
    x_pad = jnp.zeros((n_blk * MOE_BLOCK, d), h.dtype).at[dest].set(xt[tok_sorted])
    blk_e = jnp.minimum(jnp.searchsorted(pend, jnp.arange(n_blk) * MOE_BLOCK, side='right'),
                        N_EXPERTS - 1)

    def expert_block(args):
        xb, e = args
        return swiglu(xb, w1[e], w3[e], w2[e])

    y_pad = lax.map(expert_block, (x_pad.reshape(n_blk, MOE_BLOCK, d), blk_e)).reshape(-1, d)
    y_rows = y_pad[dest] * flat_g[order][:, None].astype(h.dtype)
    out = jnp.zeros((n_tok, d), h.dtype).at[tok_sorted].add(y_rows)
    return out.reshape(bsz, s, d)


def setup_inputs(seed: int = 0) -> dict:
    key = jax.random.key(seed)
    ks = jax.random.split(key, 32)
    f32 = jnp.float32

    def nrm(k, shape, scale):
        return jax.random.normal(k, shape, f32) * scale

    def gain(k, shape):
        return 1.0 + 0.02 * jax.random.normal(k, shape, f32)

    dt = jnp.exp(jax.random.uniform(ks[11], (DEPTH, N_SSD_HEADS), f32, math.log(1e-3), math.log(1e-1)))
    f_init = jax.random.uniform(ks[17], (DEPTH, N_FOX_HEADS), f32, 0.9, 0.999)
    return {
        'x': nrm(ks[0], (BATCH, SEQ, D_MODEL), 1.0),
        'attn_norm': gain(ks[1], (DEPTH, D_MODEL)),
        'w_in': nrm(ks[2], (DEPTH, D_MODEL, D_IN), D_MODEL ** -0.5),
        'nsa_q_norm': gain(ks[3], (DEPTH, HEAD_DIM)),
        'nsa_k_norm': gain(ks[4], (DEPTH, HEAD_DIM)),
        'nsa_cmp_pe_k': nrm(ks[5], (DEPTH, NSA_CMP_BLOCK, HEAD_DIM), 0.1),
        'nsa_cmp_pe_v': nrm(ks[6], (DEPTH, NSA_CMP_BLOCK, HEAD_DIM), 0.1),
        'nsa_cmp_w_k': nrm(ks[7], (DEPTH, NSA_CMP_BLOCK * HEAD_DIM, HEAD_DIM), (NSA_CMP_BLOCK * HEAD_DIM) ** -0.5),
        'nsa_cmp_w_v': nrm(ks[8], (DEPTH, NSA_CMP_BLOCK * HEAD_DIM, HEAD_DIM), (NSA_CMP_BLOCK * HEAD_DIM) ** -0.5),
        'ssd_conv_w': nrm(ks[9], (DEPTH, SSD_CONV, SSD_CONV_DIM), SSD_CONV ** -0.5),
        'ssd_conv_b': nrm(ks[10], (DEPTH, SSD_CONV_DIM), 0.02),
        'ssd_dt_bias': dt + jnp.log(-jnp.expm1(-dt)),
        'ssd_a_log': jnp.log(jax.random.uniform(ks[12], (DEPTH, N_SSD_HEADS), f32, 1.0, 16.0)),
        'ssd_d': gain(ks[13], (DEPTH, N_SSD_HEADS)),
        'ssd_norm': gain(ks[14], (DEPTH, D_SSD)),
        'fox_q_norm': gain(ks[15], (DEPTH, HEAD_DIM)),
        'fox_k_norm': gain(ks[16], (DEPTH, HEAD_DIM)),
        'fox_f_bias': jnp.log(f_init) - jnp.log1p(-f_init),
        'w_out': nrm(ks[18], (DEPTH, D_MIX, D_MODEL), D_MIX ** -0.5),
        'ffn_norm': gain(ks[19], (DEPTH, D_MODEL)),
        'ffn_w1': nrm(ks[20], (N_DENSE, D_MODEL, D_FF), D_MODEL ** -0.5),
        'ffn_w3': nrm(ks[21], (N_DENSE, D_MODEL, D_FF), D_MODEL ** -0.5),
        'ffn_w2': nrm(ks[22], (N_DENSE, D_FF, D_MODEL), D_FF ** -0.5),
        'moe_router': nrm(ks[23], (N_MOE, D_MODEL, N_EXPERTS), D_MODEL ** -0.5),
        'moe_w1': nrm(ks[24], (N_MOE, N_EXPERTS, D_MODEL, D_FF), D_MODEL ** -0.5),
        'moe_w3': nrm(ks[25], (N_MOE, N_EXPERTS, D_MODEL, D_FF), D_MODEL ** -0.5),
        'moe_w2': nrm(ks[26], (N_MOE, N_EXPERTS, D_FF, D_MODEL), D_FF ** -0.5),
    }


def reference(x, attn_norm, w_in, nsa_q_norm, nsa_k_norm, nsa_cmp_pe_k, nsa_cmp_pe_v,
              nsa_cmp_w_k, nsa_cmp_w_v, ssd_conv_w, ssd_conv_b, ssd_dt_bias, ssd_a_log, ssd_d,
              ssd_norm, fox_q_norm, fox_k_norm, fox_f_bias, w_out, ffn_norm, ffn_w1, ffn_w3,
              ffn_w2, moe_router, moe_w1, moe_w3, moe_w2):
    bsz, s, _ = x.shape
    split_pts = np.cumsum(IN_SIZES)[:-1].tolist()
    for layer in range(DEPTH):
        h = rms_norm(x, attn_norm[layer])
        proj = h @ w_in[layer]
        (q_n, kv_n, g_n, z, xbc, dt_raw, q_f, k_f, v_f, f_f) = jnp.split(proj, split_pts, axis=-1)
        o_nsa = nsa_mixer(q_n.reshape(bsz, s, N_NSA_HEADS, HEAD_DIM),
                          kv_n.reshape(bsz, s, NSA_N_KV_TENSORS, HEAD_DIM),
                          g_n.reshape(bsz, s, N_NSA_HEADS, 3),
                          nsa_q_norm[layer], nsa_k_norm[layer], nsa_cmp_pe_k[layer],
                          nsa_cmp_pe_v[layer], nsa_cmp_w_k[layer], nsa_cmp_w_v[layer])
        o_ssd = ssd_mixer(z, xbc, dt_raw, ssd_conv_w[layer], ssd_conv_b[layer], ssd_dt_bias[layer],
                          ssd_a_log[layer], ssd_d[layer], ssd_norm[layer])
        o_fox = fox_mixer(q_f.reshape(bsz, s, N_FOX_HEADS, HEAD_DIM),
                          k_f.reshape(bsz, s, N_FOX_HEADS, HEAD_DIM),
                          v_f.reshape(bsz, s, N_FOX_HEADS, HEAD_DIM),
                          f_f, fox_q_norm[layer], fox_k_norm[layer], fox_f_bias[layer])
        x = x + jnp.concatenate([o_nsa, o_ssd, o_fox], axis=-1) @ w_out[layer]
        h = rms_norm(x, ffn_norm[layer])
        idx = layer // 2
        if layer % 2 == 0:
            x = x + swiglu(h, ffn_w1[idx], ffn_w3[idx], ffn_w2[idx])
        else:
            x = x + moe_ffn(h, moe_router[idx], moe_w1[idx], moe_w3[idx], moe_w2[idx])
    return x
```

```python
import functools
import math

import jax
import jax.numpy as jnp
from jax import lax
from jax.experimental import pallas as pl
from jax.experimental.pallas import tpu as pltpu

F32 = jnp.float32
BF16 = jnp.bfloat16
I32 = jnp.int32
HIGHEST = lax.Precision.HIGHEST
NEG_INF = float("-inf")

LANES = 128
SUBLANES = 8
VMEM_BYTES_V7X = 64 * 1024 * 1024
VMEM_LIMIT = 48 * 1024 * 1024

HEAD_DIM = 64
N_NSA_HEADS = 4
N_SSD_HEADS = 8
N_FOX_HEADS = 4
D_NSA = N_NSA_HEADS * HEAD_DIM
D_SSD = N_SSD_HEADS * HEAD_DIM
D_FOX = N_FOX_HEADS * HEAD_DIM
NSA_N_KV_TENSORS = 6
NSA_CMP_BLOCK = 32
NSA_CMP_STRIDE = 16
NSA_SLC_BLOCK = 64
NSA_TOP_N = 16
NSA_WINDOW = 512
SSD_GROUPS = 2
SSD_D_STATE = 64
SSD_CONV = 4
SSD_CONV_DIM = D_SSD + 2 * SSD_GROUPS * SSD_D_STATE
N_EXPERTS = 8
TOP_K = 2
RMS_EPS = 1e-6
IN_SIZES = (D_NSA, NSA_N_KV_TENSORS * HEAD_DIM, 3 * N_NSA_HEADS,
            D_SSD, SSD_CONV_DIM, N_SSD_HEADS,
            D_FOX, D_FOX, D_FOX, N_FOX_HEADS)

C_QN, W_QN = 0, N_NSA_HEADS * LANES
C_KVN, W_KVN = C_QN + W_QN, NSA_N_KV_TENSORS * HEAD_DIM
C_Z, W_Z = C_KVN + W_KVN, D_SSD
C_XBC, W_XBC = C_Z + W_Z, SSD_CONV_DIM
C_QF, W_QF = C_XBC + W_XBC, N_FOX_HEADS * LANES
C_KF, W_KF = C_QF + W_QF, D_FOX
C_VF, W_VF = C_KF + W_KF, D_FOX
C_SM, W_SM = C_VF + W_VF, LANES
D_INP = C_SM + W_SM
SM_DT, SM_F, SM_G = 0, N_SSD_HEADS, N_SSD_HEADS + N_FOX_HEADS

TM_PROJ = 512
SSD_L = 128
FOX_TQ = 256
FOX_TK = 256
NSA_TQ = 128
NSA_TK = 256
FFN_TM = 1024
FFN_TF = 256
MOE_TM = 512
MOE_TT = 256


def _cparams(sem):
    return pltpu.CompilerParams(dimension_semantics=sem, vmem_limit_bytes=VMEM_LIMIT)


def _dot(a, b, **kw):
    return jnp.dot(a, b, preferred_element_type=F32, **kw)


def _dot_nt(a, b, **kw):
    return lax.dot_general(a, b, (((1,), (1,)), ((), ())), preferred_element_type=F32, **kw)


def _inproj_body(x_ref, g_ref, w_ref, gain_ref, sbias_ref,
                 qn_ref, kvn_ref, z_ref, xbc_ref, qf_ref, kf_ref, vf_ref, sm_ref):
    x = x_ref[...]
    ms = jnp.mean(x * x, axis=-1, keepdims=True)
    h = (x * lax.rsqrt(ms + RMS_EPS) * g_ref[...]).astype(BF16)
    lane = lax.broadcasted_iota(I32, (x.shape[0], LANES), 1)
    lo = lane < HEAD_DIM

    def proj(c0, wd):
        return _dot(h, w_ref[:, c0:c0 + wd])

    def gnorm(v, c0, modes):
        outs = []
        for s, mode in enumerate(modes):
            sl = v[:, LANES * s:LANES * (s + 1)]
            if mode is None:
                outs.append(sl)
                continue
            gain = gain_ref[:, c0 + LANES * s:c0 + LANES * (s + 1)]
            sq = sl * sl
            s_lo = jnp.sum(jnp.where(lo, sq, 0.0), axis=-1, keepdims=True)
            if mode == "both":
                s_hi = jnp.sum(jnp.where(lo, 0.0, sq), axis=-1, keepdims=True)
                msq = jnp.where(lo, s_lo, s_hi) * (1.0 / HEAD_DIM)
                r = lax.rsqrt(msq + RMS_EPS) * gain
            else:
                r = jnp.where(lo, lax.rsqrt(s_lo * (1.0 / HEAD_DIM) + RMS_EPS) * gain, 1.0)
            outs.append(sl * r)
        return jnp.concatenate(outs, axis=1)

    qn_ref[...] = gnorm(proj(C_QN, W_QN), C_QN, ["both"] * N_NSA_HEADS).astype(BF16)
    kvn_ref[...] = gnorm(proj(C_KVN, W_KVN), C_KVN, [None, "lo", "lo"]).astype(BF16)
    z_ref[...] = proj(C_Z, W_Z).astype(BF16)
    xbc_ref[...] = proj(C_XBC, W_XBC)
    qf_ref[...] = gnorm(proj(C_QF, W_QF), C_QF, ["both"] * N_FOX_HEADS).astype(BF16)
    kf_ref[...] = gnorm(proj(C_KF, W_KF), C_KF, ["both"] * (W_KF // LANES)).astype(BF16)
    vf_ref[...] = proj(C_VF, W_VF).astype(BF16)

    v = proj(C_SM, W_SM) + sbias_ref[...]
    e = jnp.exp(-jnp.abs(v))
    l1p = jnp.log1p(e)
    softplus = jnp.maximum(v, 0.0) + l1p
    log_sig = jnp.minimum(v, 0.0) - l1p
    sig = jnp.where(v >= 0.0, 1.0 / (1.0 + e), e / (1.0 + e))
    sm_ref[...] = jnp.where(lane < SM_F, softplus,
                            jnp.where(lane < SM_G, log_sig,
                                      jnp.where(lane < SM_G + 3 * N_NSA_HEADS, sig, 0.0)))


def _inproj(x2, g, w, gains, sbias):
    n, d = x2.shape
    tm = min(TM_PROJ, n)
    row = lambda wd: pl.BlockSpec((tm, wd), lambda i: (i, 0))
    full = lambda a: pl.BlockSpec(a.shape, lambda i: (0, 0))
    outs = [(W_QN, BF16), (W_KVN, BF16), (W_Z, BF16), (W_XBC, F32),
            (W_QF, BF16), (W_KF, BF16), (W_VF, BF16), (W_SM, F32)]
    return pl.pallas_call(
        _inproj_body,
        grid=(n // tm,),
        in_specs=[row(d), full(g), full(w), full(gains), full(sbias)],
        out_specs=[row(wd) for wd, _ in outs],
        out_shape=[jax.ShapeDtypeStruct((n, wd), dt) for wd, dt in outs],
        compiler_params=_cparams(("parallel",)),
        name="inproj",
    )(x2, g, w, gains, sbias)


def _pack_in_weights(w_in, nsa_q_norm, nsa_k_norm, fox_q_norm, fox_k_norm, dt_bias, f_bias):
    d = w_in.shape[0]
    pts = []
    acc = 0
    for s in IN_SIZES[:-1]:
        acc += s
        pts.append(acc)
    q_n, kv_n, g_n, z, xbc, dt, q_f, k_f, v_f, f_f = jnp.split(w_in, pts, axis=1)
    zw = jnp.zeros((d, HEAD_DIM), w_in.dtype)
    zg = jnp.zeros((HEAD_DIM,), F32)
    one = jnp.ones((HEAD_DIM,), F32)
    scale = HEAD_DIM ** -0.5
    cols, gains = [], []
    for h in range(N_NSA_HEADS):
        cols += [q_n[:, HEAD_DIM * h:HEAD_DIM * (h + 1)], zw]
        gains += [nsa_q_norm * scale, zg]
    cols.append(kv_n)
    gains += [one, one, nsa_k_norm, one, nsa_k_norm, one]
    cols += [z, xbc]
    gains += [jnp.ones((W_Z + W_XBC,), F32)]
    for h in range(N_FOX_HEADS):
        qh = q_f[:, HEAD_DIM * h:HEAD_DIM * (h + 1)]
        cols += [qh, zw] if h % 2 == 0 else [zw, qh]
        gains += [fox_q_norm * scale, zg] if h % 2 == 0 else [zg, fox_q_norm * scale]
    cols += [k_f, v_f]
    gains += [jnp.tile(fox_k_norm, N_FOX_HEADS), jnp.ones((W_VF,), F32)]
    pad = W_SM - (N_SSD_HEADS + N_FOX_HEADS + 3 * N_NSA_HEADS)
    cols += [dt, f_f, g_n, jnp.zeros((d, pad), w_in.dtype)]
    gains += [jnp.ones((W_SM,), F32)]
    w = jnp.concatenate(cols, axis=1).astype(BF16)
    gain = jnp.concatenate([g.astype(F32) for g in gains]).reshape(1, D_INP)
    sbias = jnp.concatenate([dt_bias.astype(F32), f_bias.astype(F32),
                             jnp.zeros((W_SM - N_SSD_HEADS - N_FOX_HEADS,), F32)]).reshape(1, W_SM)
    return w, gain, sbias


def _ssd_body(xbc_ref, z_ref, sm_ref, cw_ref, alog_ref, dexp_ref, ng_ref, e_ref, bm_ref,
              o_ref, ccol_ref, crow_ref, xw_ref, st_ref, fc_ref):
    L = xbc_ref.shape[1]
    c = pl.program_id(1)

    @pl.when(c == 0)
    def _():
        xw_ref[0:SUBLANES, :] = jnp.zeros((SUBLANES, W_XBC), F32)
        st_ref[...] = jnp.zeros_like(st_ref)
        fc_ref[...] = jnp.zeros_like(fc_ref)

    xw_ref[SUBLANES:SUBLANES + L, :] = xbc_ref[0]
    acc = jnp.broadcast_to(cw_ref[SSD_CONV:SSD_CONV + 1, :], (L, W_XBC))
    for k in range(SSD_CONV):
        r0 = SUBLANES - (SSD_CONV - 1) + k
        acc = acc + cw_ref[k:k + 1, :] * xw_ref[r0:r0 + L, :]
    xw_ref[0:SUBLANES, :] = xw_ref[L:L + SUBLANES, :]
    u = acc * (1.0 / (1.0 + jnp.exp(-acc)))
    xs = u[:, 0:D_SSD]
    bmat = u[:, D_SSD:D_SSD + LANES]
    cmat = u[:, D_SSD + LANES:D_SSD + 2 * LANES]

    lane = lax.broadcasted_iota(I32, (L, LANES), 1)
    sm = sm_ref[0]
    a_row = jnp.where(lane[0:1] < N_SSD_HEADS, -jnp.exp(alog_ref[...]), 0.0)
    is_dt = lane < SM_F
    is_f = (lane >= SM_F) & (lane < SM_G)
    vals = jnp.where(is_dt, sm * a_row, jnp.where(is_f, sm, 0.0))
    ri = lax.broadcasted_iota(I32, (L, L), 0)
    ci = lax.broadcasted_iota(I32, (L, L), 1)
    causal = ci <= ri
    tril = jnp.where(causal, 1.0, 0.0).astype(F32)
    cs_all = _dot(tril, vals, precision=HIGHEST)

    ccol = jnp.where(is_f, cs_all + fc_ref[...], 0.0)
    fc_ref[...] = ccol[L - 1:L, :]
    ccol_ref[0] = ccol
    crow_ref[0] = ccol.T[SM_F:SM_F + SUBLANES, :]

    cs = jnp.where(is_dt, cs_all, 0.0)
    cs_t = cs.T
    cs_exp = _dot(cs, e_ref[...], precision=HIGHEST)
    dt_exp = _dot(jnp.where(is_dt, sm, 0.0), e_ref[...], precision=HIGHEST)
    xdt = xs * dt_exp
    decay_out = jnp.exp(cs_exp)
    cs_last = cs_exp[L - 1:L, :]
    dte = jnp.exp(cs_last - cs_exp)
    chunk_decay = jnp.exp(cs_last)

    lo = lane < HEAD_DIM
    b_bf = bmat.astype(BF16)
    xdt_bf = xdt.astype(BF16)
    c_bf = cmat.astype(BF16)
    sc = [_dot_nt(jnp.where(lo, cmat, 0.0).astype(BF16), b_bf),
          _dot_nt(jnp.where(lo, 0.0, cmat).astype(BF16), b_bf)]
    y_pairs = []
    for pr in range(N_SSD_HEADS // 2):
        ys = []
        for hh in range(2):
            h = 2 * pr + hh
            grp = h // (N_SSD_HEADS // SSD_GROUPS)
            seg = cs[:, h:h + 1] - cs_t[h:h + 1, :]
            lmat = jnp.exp(jnp.where(causal, seg, NEG_INF))
            m = (sc[grp] * lmat).astype(BF16)
            ys.append(_dot(m, xdt_bf[:, LANES * pr:LANES * (pr + 1)]))
        y_pairs.append(jnp.where(lo, ys[0], ys[1]))
    y_diag = jnp.concatenate(y_pairs, axis=1)

    state = st_ref[...]
    y_off = _dot(c_bf, state.astype(BF16)) * decay_out
    bt = bmat.T.astype(BF16)
    st_new = _dot(bt, (xdt * dte).astype(BF16))
    st_ref[...] = state * chunk_decay + bm_ref[...] * st_new

    y = y_diag + y_off + xs * dexp_ref[...]
    zf = z_ref[0].astype(F32)
    y = y * (zf * (1.0 / (1.0 + jnp.exp(-zf))))
    ms = jnp.mean(y * y, axis=-1, keepdims=True)
    o_ref[0] = (y * lax.rsqrt(ms + RMS_EPS) * ng_ref[...]).astype(BF16)


def _ssd(xbc, z, sm, conv_w, conv_b, a_log, d_skip, norm_g):
    b, s, _ = xbc.shape
    L = min(SSD_L, s)
    cw = jnp.concatenate([conv_w.astype(F32), conv_b.astype(F32)[None, :],
                          jnp.zeros((SUBLANES - SSD_CONV - 1, W_XBC), F32)], axis=0)
    alog = jnp.concatenate([a_log.astype(F32), jnp.zeros((LANES - N_SSD_HEADS,), F32)]).reshape(1, LANES)
    dexp = jnp.repeat(d_skip.astype(F32), HEAD_DIM).reshape(1, D_SSD)
    ng = norm_g.astype(F32).reshape(1, D_SSD)
    hid = jnp.arange(LANES)[:, None]
    col = jnp.arange(D_SSD)[None, :]
    emat = ((col // HEAD_DIM) == hid).astype(F32)
    heads_per_group = N_SSD_HEADS // SSD_GROUPS
    bmask = ((hid // SSD_D_STATE) == (col // (HEAD_DIM * heads_per_group))).astype(F32)
    blk = lambda wd: pl.BlockSpec((1, L, wd), lambda bi, ci: (bi, ci, 0))
    full = lambda a: pl.BlockSpec(a.shape, lambda bi, ci: (0, 0))
    return pl.pallas_call(
        _ssd_body,
        grid=(b, s // L),
        in_specs=[blk(W_XBC), blk(D_SSD), blk(LANES), full(cw), full(alog), full(dexp), full(ng),
                  full(emat), full(bmask)],
        out_specs=[blk(D_SSD), blk(LANES), pl.BlockSpec((1, SUBLANES, L), lambda bi, ci: (bi, 0, ci))],
        out_shape=[jax.ShapeDtypeStruct((b, s, D_SSD), BF16),
                   jax.ShapeDtypeStruct((b, s, LANES), F32),
                   jax.ShapeDtypeStruct((b, SUBLANES, s), F32)],
        scratch_shapes=[pltpu.VMEM((L + SUBLANES, W_XBC), F32),
                        pltpu.VMEM((LANES, D_SSD), F32),
                        pltpu.VMEM((1, LANES), F32)],
        compiler_params=_cparams(("parallel", "arbitrary")),
        name="ssd_scan",
    )(xbc, z, sm, cw, alog, dexp, ng, emat, bmask)


def _fox_body(q_ref, k_ref, v_ref, ccol_ref, crow_ref, o_ref, m_ref, l_ref, acc_ref):
    tq = q_ref.shape[1]
    tk = tq
    qi = pl.program_id(1)
    m_ref[...] = jnp.full_like(m_ref, NEG_INF)
    l_ref[...] = jnp.zeros_like(l_ref)
    acc_ref[...] = jnp.zeros_like(acc_ref)
    ccol = ccol_ref[0]
    ri = lax.broadcasted_iota(I32, (tq, tk), 0)
    ci = lax.broadcasted_iota(I32, (tq, tk), 1)
    diag_mask = ci <= ri

    def tile(j, masked):
        k0 = pl.multiple_of(j * tk, tk)
        for pr in range(N_FOX_HEADS // 2):
            kslab = k_ref[0, pl.ds(k0, tk), LANES * pr:LANES * (pr + 1)]
            vslab = v_ref[0, pl.ds(k0, tk), LANES * pr:LANES * (pr + 1)]
            for hh in range(2):
                h = 2 * pr + hh
                q = q_ref[0, :, LANES * h:LANES * (h + 1)]
                s = _dot_nt(q, kslab)
                s = s + (ccol[:, SM_F + h:SM_F + h + 1] - crow_ref[0, h:h + 1, pl.ds(k0, tk)])
                if masked:
                    s = jnp.where(diag_mask, s, NEG_INF)
                m_prev = m_ref[h]
                m_new = jnp.maximum(m_prev, jnp.max(s, axis=-1, keepdims=True))
                alpha = jnp.exp(m_prev - m_new)
                p = jnp.exp(s - m_new)
                l_ref[h] = alpha * l_ref[h] + jnp.sum(p, axis=-1, keepdims=True)
                acc_ref[h] = alpha * acc_ref[h] + _dot(p.astype(BF16), vslab)
                m_ref[h] = m_new

    def body(j, carry):
        tile(j, False)
        return carry

    lax.fori_loop(0, qi, body, 0)
    tile(qi, True)

    lane = lax.broadcasted_iota(I32, (tq, LANES), 1)
    lo = lane < HEAD_DIM
    outs = []
    for pr in range(N_FOX_HEADS // 2):
        tiny = jnp.finfo(F32).tiny
        o0 = acc_ref[2 * pr] / jnp.maximum(l_ref[2 * pr], tiny)
        o1 = acc_ref[2 * pr + 1] / jnp.maximum(l_ref[2 * pr + 1], tiny)
        outs.append(jnp.where(lo, o0, o1))
    o_ref[0] = jnp.concatenate(outs, axis=1).astype(BF16)


def _fox(qf, kf, vf, ccol, crow):
    b, s, _ = qf.shape
    tq = min(FOX_TQ, s)
    return pl.pallas_call(
        _fox_body,
        grid=(b, s // tq),
        in_specs=[pl.BlockSpec((1, tq, W_QF), lambda bi, qi: (bi, qi, 0)),
                  pl.BlockSpec((1, s, W_KF), lambda bi, qi: (bi, 0, 0)),
                  pl.BlockSpec((1, s, W_VF), lambda bi, qi: (bi, 0, 0)),
                  pl.BlockSpec((1, tq, LANES), lambda bi, qi: (bi, qi, 0)),
                  pl.BlockSpec((1, SUBLANES, s), lambda bi, qi: (bi, 0, 0))],
        out_specs=pl.BlockSpec((1, tq, D_FOX), lambda bi, qi: (bi, qi, 0)),
        out_shape=jax.ShapeDtypeStruct((b, s, D_FOX), BF16),
        scratch_shapes=[pltpu.VMEM((N_FOX_HEADS, tq, 1), F32),
                        pltpu.VMEM((N_FOX_HEADS, tq, 1), F32),
                        pltpu.VMEM((N_FOX_HEADS, tq, LANES), F32)],
        compiler_params=_cparams(("parallel", "arbitrary")),
        name="fox_attn",
    )(qf, kf, vf, ccol, crow)


def _cmp_body(gk_ref, gv_ref, wkt_ref, wkb_ref, wvt_ref, wvb_ref, pe_ref, gain_ref, o_ref):
    ncp = gk_ref.shape[1]
    gk = gk_ref[0]
    gv = gv_ref[0]
    top = _dot(gk, wkt_ref[...]) + _dot(gv, wvt_ref[...])
    bot = _dot(gk, wkb_ref[...]) + _dot(gv, wvb_ref[...])
    bot_next = pltpu.roll(bot, ncp - 1, 0)
    raw = top + bot_next + pe_ref[0:1, :]
    lane = lax.broadcasted_iota(I32, raw.shape, 1)
    lo = lane < HEAD_DIM
    msq = jnp.sum(jnp.where(lo, raw * raw, 0.0), axis=-1, keepdims=True) * (1.0 / HEAD_DIM)
    out = jnp.where(lo, raw * lax.rsqrt(msq + RMS_EPS) * gain_ref[...], raw)
    row = lax.broadcasted_iota(I32, raw.shape, 0)
    o_ref[0] = jnp.where(row < ncp - 1, out, 0.0).astype(BF16)


def _nsa_compress(kvn, pe_k, pe_v, w_ck, w_cv, k_norm):
    b, s, _ = kvn.shape
    half = NSA_CMP_STRIDE * HEAD_DIM
    ng = s // NSA_CMP_STRIDE
    gk = kvn[:, :, 0:HEAD_DIM].reshape(b, ng, half)
    gv = kvn[:, :, HEAD_DIM:2 * HEAD_DIM].reshape(b, ng, half)
    zc = jnp.zeros((half, HEAD_DIM), F32)
    wk = w_ck.astype(F32)
    wv = w_cv.astype(F32)
    wkt = jnp.concatenate([wk[:half], zc], axis=1).astype(BF16)
    wkb = jnp.concatenate([wk[half:], zc], axis=1).astype(BF16)
    wvt = jnp.concatenate([zc, wv[:half]], axis=1).astype(BF16)
    wvb = jnp.concatenate([zc, wv[half:]], axis=1).astype(BF16)
    pe_const = jnp.concatenate([jnp.dot(pe_k.reshape(1, -1).astype(F32), wk, precision=HIGHEST),
                                jnp.dot(pe_v.reshape(1, -1).astype(F32), wv, precision=HIGHEST)], axis=1)
    pe_const = jnp.concatenate([pe_const, jnp.zeros((SUBLANES - 1, LANES), F32)], axis=0)
    gain = jnp.concatenate([k_norm.astype(F32), jnp.ones((HEAD_DIM,), F32)]).reshape(1, LANES)
    blk = pl.BlockSpec((1, ng, half), lambda bi: (bi, 0, 0))
    full = lambda a: pl.BlockSpec(a.shape, lambda bi: (0, 0))
    return pl.pallas_call(
        _cmp_body,
        grid=(b,),
        in_specs=[blk, blk, full(wkt), full(wkb), full(wvt), full(wvb), full(pe_const), full(gain)],
        out_specs=pl.BlockSpec((1, ng, LANES), lambda bi: (bi, 0, 0)),
        out_shape=jax.ShapeDtypeStruct((b, ng, LANES), BF16),
        compiler_params=_cparams(("parallel",)),
        name="nsa_compress",
    )(gk, gv, wkt, wkb, wvt, wvb, pe_const, gain)


def _nsa_body(q_ref, kv_ref, kc_ref, sm_ref, ovt_ref, e_ref, o_ref,
              qs_ref, sc_ref, m_ref, l_ref, acc_ref, *, n_slc, top_n):
    tq = q_ref.shape[1]
    tk = e_ref.shape[2]
    ncp = kc_ref.shape[1]
    nh = N_NSA_HEADS
    rows = nh * tq
    qi = pl.program_id(1)
    t0 = qi * tq
    tiny = jnp.finfo(F32).tiny

    for h in range(nh):
        qs_ref[h * tq:(h + 1) * tq, :] = q_ref[0, :, LANES * h:LANES * (h + 1)]
    qs = qs_ref[...]

    kc = kc_ref[0]
    s_c = _dot_nt(qs, kc).reshape(nh, tq, ncp)
    t_q = t0 + lax.broadcasted_iota(I32, (1, tq, ncp), 1)
    c_i = lax.broadcasted_iota(I32, (1, tq, ncp), 2)
    valid_c = (c_i * NSA_CMP_STRIDE + (NSA_CMP_BLOCK - 1) <= t_q) & (c_i < ncp - 1)
    s_c = jnp.where(valid_c, s_c, NEG_INF)
    m_c = jnp.max(s_c, axis=-1, keepdims=True)
    e_c = jnp.exp(s_c - jnp.where(m_c == NEG_INF, 0.0, m_c))
    p_c = e_c / jnp.maximum(jnp.sum(e_c, axis=-1, keepdims=True), tiny)
    o_c = _dot(p_c.reshape(rows, ncp).astype(BF16), kc)

    p_sum = jnp.sum(p_c, axis=0)
    imp_t = _dot_nt(ovt_ref[...], p_sum, precision=HIGHEST)
    n_i = lax.broadcasted_iota(I32, (LANES, tq), 0)
    t_l = t0 + lax.broadcasted_iota(I32, (LANES, tq), 1)
    avail = n_i * NSA_SLC_BLOCK <= t_l
    forced = (n_i * NSA_SLC_BLOCK <= t_l) & (t_l < (n_i + 1) * NSA_SLC_BLOCK) | (n_i == 0)
    score = jnp.where(avail, jnp.where(forced, jnp.inf, imp_t), NEG_INF)
    sc_ref[...] = score
    ns_pad = ((n_slc + SUBLANES - 1) // SUBLANES) * SUBLANES
    sc = sc_ref[0:ns_pad, :]
    n_s = n_i[0:ns_pad, :]
    cnt = jnp.zeros((ns_pad, tq), F32)
    for mth in range(n_slc):
        r = sc_ref[mth:mth + 1, :]
        cnt = cnt + jnp.where(r > sc, 1.0, jnp.where((r == sc) & (n_s > mth), 1.0, 0.0))
    sel_t = jnp.where(avail[0:ns_pad, :] & (cnt < float(top_n)), 1.0, 0.0)
    sc_ref[0:ns_pad, :] = sel_t
    if ns_pad < LANES:
        sc_ref[ns_pad:LANES, :] = jnp.zeros((LANES - ns_pad, tq), F32)
    sel = sc_ref[...].T.astype(BF16)

    t_row = t0 + lax.broadcasted_iota(I32, (1, tq, tk), 1)
    k_col = lax.broadcasted_iota(I32, (1, tq, tk), 2)

    def attend(br, j, slab, mask):
        k0 = pl.multiple_of(j * tk, tk)
        kv = kv_ref[0, pl.ds(k0, tk), LANES * slab:LANES * (slab + 1)]
        s = _dot_nt(qs, kv).reshape(nh, tq, tk) + jnp.where(mask, 0.0, NEG_INF)
        m_prev = m_ref[br]
        m_new = jnp.maximum(m_prev, jnp.max(s, axis=-1, keepdims=True))
        m_safe = jnp.where(m_new == NEG_INF, 0.0, m_new)
        alpha = jnp.exp(m_prev - m_safe)
        p = jnp.exp(s - m_safe)
        l_ref[br] = alpha * l_ref[br] + jnp.sum(p, axis=-1, keepdims=True)
        pv = _dot(p.reshape(rows, tk).astype(BF16), kv).reshape(nh, tq, LANES)
        acc_ref[br] = alpha * acc_ref[br] + pv
        m_ref[br] = m_new

    m_ref[...] = jnp.full_like(m_ref, NEG_INF)
    l_ref[...] = jnp.zeros_like(l_ref)
    acc_ref[...] = jnp.zeros_like(acc_ref)

    def sel_body(j, carry):
        pos = j * tk + k_col
        blk_mask = _dot(sel, e_ref[j]) > 0.5
        attend(0, j, 1, blk_mask[None] & (pos <= t_row))
        return carry

    lax.fori_loop(0, (t0 + tq + tk - 1) // tk, sel_body, 0)

    def win_body(j, carry):
        pos = j * tk + k_col
        attend(1, j, 2, (pos <= t_row) & (pos > t_row - NSA_WINDOW))
        return carry

    w_lo = jnp.maximum(t0 - (NSA_WINDOW - 1), 0) // tk
    lax.fori_loop(w_lo, (t0 + tq + tk - 1) // tk, win_body, 0)

    sm = sm_ref[0]
    o_c3 = o_c.reshape(nh, tq, LANES)
    lane = lax.broadcasted_iota(I32, (tq, LANES), 1)
    lo = lane < HEAD_DIM
    heads = []
    for h in range(nh):
        g0 = sm[:, SM_G + 3 * h:SM_G + 3 * h + 1]
        g1 = sm[:, SM_G + 3 * h + 1:SM_G + 3 * h + 2]
        g2 = sm[:, SM_G + 3 * h + 2:SM_G + 3 * h + 3]
        o_s = acc_ref[0, h] / jnp.maximum(l_ref[0, h], tiny)
        o_w = acc_ref[1, h] / jnp.maximum(l_ref[1, h], tiny)
        heads.append(g0 * o_c3[h] + g1 * o_s + g2 * o_w)
    outs = []
    for pr in range(nh // 2):
        outs.append(jnp.where(lo, pltpu.roll(heads[2 * pr], HEAD_DIM, 1), heads[2 * pr + 1]))
    o_ref[0] = jnp.concatenate(outs, axis=1).astype(BF16)


def _nsa(qn, kvn, kc, sm):
    b, s, _ = qn.shape
    tq = min(NSA_TQ, s)
    tk = min(NSA_TK, s)
    n_cmp = (s - NSA_CMP_BLOCK) // NSA_CMP_STRIDE + 1
    n_slc = s // NSA_SLC_BLOCK
    ncp = kc.shape[1]
    assert n_cmp == ncp - 1 and n_slc <= LANES
    top_n = min(NSA_TOP_N, n_slc)
    c_start = jnp.arange(ncp) * NSA_CMP_STRIDE
    n_start = jnp.arange(LANES) * NSA_SLC_BLOCK
    ovt = ((c_start[None, :] < n_start[:, None] + NSA_SLC_BLOCK)
           & (c_start[None, :] + NSA_CMP_BLOCK > n_start[:, None])
           & (jnp.arange(ncp)[None, :] < n_cmp) & (jnp.arange(LANES)[:, None] < n_slc)).astype(F32)
    pos = jnp.arange(s).reshape(s // tk, 1, tk)
    emat = ((pos // NSA_SLC_BLOCK) == jnp.arange(LANES)[None, :, None]).astype(BF16)
    rows = N_NSA_HEADS * tq
    return pl.pallas_call(
        functools.partial(_nsa_body, n_slc=n_slc, top_n=top_n),
        grid=(b, s // tq),
        in_specs=[pl.BlockSpec((1, tq, W_QN), lambda bi, qi: (bi, qi, 0)),
                  pl.BlockSpec((1, s, W_KVN), lambda bi, qi: (bi, 0, 0)),
                  pl.BlockSpec((1, ncp, LANES), lambda bi, qi: (bi, 0, 0)),
                  pl.BlockSpec((1, tq, LANES), lambda bi, qi: (bi, qi, 0)),
                  pl.BlockSpec(ovt.shape, lambda bi, qi: (0, 0)),
                  pl.BlockSpec(emat.shape, lambda bi, qi: (0, 0, 0))],
        out_specs=pl.BlockSpec((1, tq, D_NSA), lambda bi, qi: (bi, qi, 0)),
        out_shape=jax.ShapeDtypeStruct((b, s, D_NSA), BF16),
        scratch_shapes=[pltpu.VMEM((rows, LANES), BF16),
                        pltpu.VMEM((LANES, tq), F32),
                        pltpu.VMEM((2, N_NSA_HEADS, tq, 1), F32),
                        pltpu.VMEM((2, N_NSA_HEADS, tq, 1), F32),
                        pltpu.VMEM((2, N_NSA_HEADS, tq, LANES), F32)],
        compiler_params=_cparams(("parallel", "arbitrary")),
        name="nsa_attn",
    )(qn, kvn, kc, sm, ovt, emat)


def _outproj_body(*refs, route):
    if route:
        (on_ref, os_ref, of_ref, x_ref, wn_ref, ws_ref, wf_ref, g_ref, wr_ref,
         xo_ref, h_ref, rt_ref) = refs
    else:
        on_ref, os_ref, of_ref, x_ref, wn_ref, ws_ref, wf_ref, g_ref, xo_ref, h_ref = refs
    x = (x_ref[...] + _dot(on_ref[...], wn_ref[...]) + _dot(os_ref[...], ws_ref[...])
         + _dot(of_ref[...], wf_ref[...]))
    xo_ref[...] = x
    ms = jnp.mean(x * x, axis=-1, keepdims=True)
    h = x * lax.rsqrt(ms + RMS_EPS) * g_ref[...]
    h_ref[...] = h.astype(h_ref.dtype)
    if route:
        logits = _dot(h, wr_ref[...], precision=HIGHEST)
        lane = lax.broadcasted_iota(I32, logits.shape, 1)
        lf = lane.astype(F32)
        lg = jnp.where(lane < N_EXPERTS, logits, NEG_INF)
        m1 = jnp.max(lg, axis=-1, keepdims=True)
        i1 = jnp.min(jnp.where(lg == m1, lf, float(LANES)), axis=-1, keepdims=True)
        lg2 = jnp.where(lf == i1, NEG_INF, lg)
        m2 = jnp.max(lg2, axis=-1, keepdims=True)
        i2 = jnp.min(jnp.where(lg2 == m2, lf, float(LANES)), axis=-1, keepdims=True)
        e21 = jnp.exp(m2 - m1)
        g1 = 1.0 / (1.0 + e21)
        g2 = e21 / (1.0 + e21)
        rt_ref[...] = jnp.where(lane == 0, i1, jnp.where(lane == 1, i2,
                                jnp.where(lane == 2, g1, jnp.where(lane == 3, g2, 0.0))))


def _outproj(o_nsa, o_ssd, o_fox, x2, w_out, g, w_router=None):
    n, d = x2.shape
    tm = min(TM_PROJ, n)
    route = w_router is not None
    wn = w_out[0:D_NSA].astype(BF16)
    ws = w_out[D_NSA:D_NSA + D_SSD].astype(BF16)
    wf = w_out[D_NSA + D_SSD:].astype(BF16)
    gg = g.astype(F32).reshape(1, d)
    row = lambda wd: pl.BlockSpec((tm, wd), lambda i: (i, 0))
    full = lambda a: pl.BlockSpec(a.shape, lambda i: (0, 0))
    ins = [o_nsa, o_ssd, o_fox, x2, wn, ws, wf, gg]
    in_specs = [row(D_NSA), row(D_SSD), row(D_FOX), row(d), full(wn), full(ws), full(wf), full(gg)]
    out_specs = [row(d), row(d)]
    out_shape = [jax.ShapeDtypeStruct((n, d), F32),
                 jax.ShapeDtypeStruct((n, d), F32 if route else BF16)]
    if route:
        wr = jnp.concatenate([w_router.astype(F32), jnp.zeros((d, LANES - N_EXPERTS), F32)], axis=1)
        ins.append(wr)
        in_specs.append(full(wr))
        out_specs.append(row(LANES))
        out_shape.append(jax.ShapeDtypeStruct((n, LANES), F32))
    return pl.pallas_call(
        functools.partial(_outproj_body, route=route),
        grid=(n // tm,),
        in_specs=in_specs, out_specs=out_specs, out_shape=out_shape,
        compiler_params=_cparams(("parallel",)),
        name="outproj_route" if route else "outproj",
    )(*ins)


def _swiglu_acc(h, w1_ref, w3_ref, w2_ref, acc_ref, f):
    @pl.when(f == 0)
    def _():
        acc_ref[...] = jnp.zeros_like(acc_ref)

    a = _dot(h, w1_ref[...])
    b = _dot(h, w3_ref[...])
    mid = (a * (1.0 / (1.0 + jnp.exp(-a))) * b).astype(BF16)
    acc_ref[...] += _dot(mid, w2_ref[...])


def _ffn_body(h_ref, x_ref, w1_ref, w3_ref, w2_ref, o_ref, acc_ref):
    f = pl.program_id(1)
    _swiglu_acc(h_ref[...], w1_ref, w3_ref, w2_ref, acc_ref, f)

    @pl.when(f == pl.num_programs(1) - 1)
    def _():
        o_ref[...] = x_ref[...] + acc_ref[...]


def _ffn_dense(h, x2, w1, w3, w2):
    n, d = x2.shape
    ff = w1.shape[1]
    tm = min(FFN_TM, n)
    tf = FFN_TF
    return pl.pallas_call(
        _ffn_body,
        grid=(n // tm, ff // tf),
        in_specs=[pl.BlockSpec((tm, d), lambda i, f: (i, 0)),
                  pl.BlockSpec((tm, d), lambda i, f: (i, 0)),
                  pl.BlockSpec((d, tf), lambda i, f: (0, f)),
                  pl.BlockSpec((d, tf), lambda i, f: (0, f)),
                  pl.BlockSpec((tf, d), lambda i, f: (f, 0))],
        out_specs=pl.BlockSpec((tm, d), lambda i, f: (i, 0)),
        out_shape=jax.ShapeDtypeStruct((n, d), F32),
        scratch_shapes=[pltpu.VMEM((tm, d), F32)],
        compiler_params=_cparams(("parallel", "arbitrary")),
        name="ffn_dense",
    )(h, x2, w1.astype(BF16), w3.astype(BF16), w2.astype(BF16))


def _gffn_body(be_ref, xs_ref, w1_ref, w3_ref, w2_ref, o_ref, acc_ref):
    f = pl.program_id(1)
    _swiglu_acc(xs_ref[...].astype(BF16), w1_ref.at[0], w3_ref.at[0], w2_ref.at[0], acc_ref, f)

    @pl.when(f == pl.num_programs(1) - 1)
    def _():
        o_ref[...] = acc_ref[...]


def _ffn_grouped(xs, blk_e, w1, w3, w2):
    p, d = xs.shape
    ff = w1.shape[2]
    tm = MOE_TM
    tf = FFN_TF
    grid_spec = pltpu.PrefetchScalarGridSpec(
        num_scalar_prefetch=1,
        grid=(p // tm, ff // tf),
        in_specs=[pl.BlockSpec((tm, d), lambda i, f, be: (i, 0)),
                  pl.BlockSpec((1, d, tf), lambda i, f, be: (be[i], 0, f)),
                  pl.BlockSpec((1, d, tf), lambda i, f, be: (be[i], 0, f)),
                  pl.BlockSpec((1, tf, d), lambda i, f, be: (be[i], f, 0))],
        out_specs=pl.BlockSpec((tm, d), lambda i, f, be: (i, 0)),
        scratch_shapes=[pltpu.VMEM((tm, d), F32)])
    return pl.pallas_call(
        _gffn_body,
        grid_spec=grid_spec,
        out_shape=jax.ShapeDtypeStruct((p, d), F32),
        compiler_params=_cparams(("parallel", "arbitrary")),
        name="ffn_grouped",
    )(blk_e, xs, w1.astype(BF16), w3.astype(BF16), w2.astype(BF16))


def _row_copy(src, dst, sem):
    return pltpu.make_async_copy(src, dst, sem)


def _dispatch_body(dest_ref, h_hbm, xs_in, xs_hbm, sem):
    del xs_in
    tt = dest_ref.shape[2]
    base = pl.program_id(0) * tt

    def issue(r, carry):
        for k in range(TOP_K):
            d = dest_ref[0, k, r]
            _row_copy(h_hbm.at[pl.ds(base + r, 1)], xs_hbm.at[pl.ds(d, 1)], sem).start()
        return carry

    lax.fori_loop(0, tt, issue, 0)

    def drain(r, carry):
        for k in range(TOP_K):
            _row_copy(h_hbm.at[pl.ds(0, 1)], xs_hbm.at[pl.ds(0, 1)], sem).wait()
        return carry

    lax.fori_loop(0, tt, drain, 0)


def _dispatch(h, dest3, p):
    n, d = h.shape
    tt = dest3.shape[2]
    xs0 = jnp.zeros((p, d), h.dtype)
    return pl.pallas_call(
        _dispatch_body,
        grid=(n // tt,),
        in_specs=[pl.BlockSpec((1, TOP_K, tt), lambda i: (i, 0, 0), memory_space=pltpu.SMEM),
                  pl.BlockSpec(memory_space=pl.ANY),
                  pl.BlockSpec(memory_space=pl.ANY)],
        out_specs=pl.BlockSpec(memory_space=pl.ANY),
        out_shape=jax.ShapeDtypeStruct((p, d), h.dtype),
        scratch_shapes=[pltpu.SemaphoreType.DMA(())],
        input_output_aliases={2: 0},
        compiler_params=_cparams(("arbitrary",)),
        name="moe_dispatch",
    )(dest3, h, xs0)


def _combine_body(dest_ref, x_ref, rt_ref, y_hbm, o_ref, buf_ref, sem):
    tt = x_ref.shape[0]

    def issue(r, carry):
        for k in range(TOP_K):
            d = dest_ref[0, k, r]
            _row_copy(y_hbm.at[pl.ds(d, 1)], buf_ref.at[k, pl.ds(r, 1)], sem).start()
        return carry

    lax.fori_loop(0, tt, issue, 0)

    def drain(r, carry):
        for k in range(TOP_K):
            _row_copy(y_hbm.at[pl.ds(0, 1)], buf_ref.at[k, pl.ds(0, 1)], sem).wait()
        return carry

    lax.fori_loop(0, tt, drain, 0)
    rt = rt_ref[...]
    o_ref[...] = x_ref[...] + rt[:, 2:3] * buf_ref[0] + rt[:, 3:4] * buf_ref[1]


def _combine(x2, route, dest3, y):
    n, d = x2.shape
    tt = dest3.shape[2]
    return pl.pallas_call(
        _combine_body,
        grid=(n // tt,),
        in_specs=[pl.BlockSpec((1, TOP_K, tt), lambda i: (i, 0, 0), memory_space=pltpu.SMEM),
                  pl.BlockSpec((tt, d), lambda i: (i, 0)),
                  pl.BlockSpec((tt, LANES), lambda i: (i, 0)),
                  pl.BlockSpec(memory_space=pl.ANY)],
        out_specs=pl.BlockSpec((tt, d), lambda i: (i, 0)),
        out_shape=jax.ShapeDtypeStruct((n, d), F32),
        scratch_shapes=[pltpu.VMEM((TOP_K, tt, d), F32), pltpu.SemaphoreType.DMA(())],
        compiler_params=_cparams(("arbitrary",)),
        name="moe_combine",
    )(dest3, x2, route, y)


def _moe(h, x2, route, w1, w3, w2):
    n, d = x2.shape
    tm = MOE_TM
    tt = min(MOE_TT, n)
    n_rows = n * TOP_K
    flat_e = route[:, 0:TOP_K].astype(I32).reshape(-1)
    onehot = (flat_e[:, None] == jnp.arange(N_EXPERTS)[None, :]).astype(I32)
    csum = jnp.cumsum(onehot, axis=0)
    rank = jnp.sum(csum * onehot, axis=1) - 1
    counts = csum[-1]
    padded = ((counts + tm - 1) // tm) * tm
    pend = jnp.cumsum(padded)
    pstart = pend - padded
    dest = jnp.sum(onehot * pstart[None, :], axis=1) + rank
    n_blk = -(-n_rows // tm) + N_EXPERTS
    blk_e = jnp.minimum(jnp.searchsorted(pend, jnp.arange(n_blk) * tm, side="right"),
                        N_EXPERTS - 1).astype(I32)
    dest3 = dest.reshape(n // tt, tt, TOP_K).transpose(0, 2, 1).astype(I32)
    xs = _dispatch(h, dest3, n_blk * tm)
    y = _ffn_grouped(xs, blk_e, w1, w3, w2)
    return _combine(x2, route, dest3, y)


def kernel(x, attn_norm, w_in, nsa_q_norm, nsa_k_norm, nsa_cmp_pe_k, nsa_cmp_pe_v, nsa_cmp_w_k,
           nsa_cmp_w_v, ssd_conv_w, ssd_conv_b, ssd_dt_bias, ssd_a_log, ssd_d, ssd_norm, fox_q_norm,
           fox_k_norm, fox_f_bias, w_out, ffn_norm, ffn_w1, ffn_w3, ffn_w2, moe_router, moe_w1,
           moe_w3, moe_w2):
    b, s, d = x.shape
    n = b * s
    depth = w_in.shape[0]
    x2 = x.reshape(n, d).astype(F32)
    for layer in range(depth):
        w, gains, sbias = _pack_in_weights(w_in[layer], nsa_q_norm[layer], nsa_k_norm[layer],
                                           fox_q_norm[layer], fox_k_norm[layer],
                                           ssd_dt_bias[layer], fox_f_bias[layer])
        g = attn_norm[layer].astype(F32).reshape(1, d)
        qn, kvn, z, xbc, qf, kf, vf, sm = _inproj(x2, g, w, gains, sbias)
        r3 = lambda a: a.reshape(b, s, a.shape[-1])
        kvn3, sm3 = r3(kvn), r3(sm)
        kc = _nsa_compress(kvn3, nsa_cmp_pe_k[layer], nsa_cmp_pe_v[layer],
                           nsa_cmp_w_k[layer], nsa_cmp_w_v[layer], nsa_k_norm[layer])
        o_nsa = _nsa(r3(qn), kvn3, kc, sm3)
        o_ssd, ccol, crow = _ssd(r3(xbc), r3(z), sm3, ssd_conv_w[layer], ssd_conv_b[layer],
                                 ssd_a_log[layer], ssd_d[layer], ssd_norm[layer])
        o_fox = _fox(r3(qf), r3(kf), r3(vf), ccol, crow)
        idx = layer // 2
        is_moe = layer % 2 == 1
        res = _outproj(o_nsa.reshape(n, D_NSA), o_ssd.reshape(n, D_SSD), o_fox.reshape(n, D_FOX),
                       x2, w_out[layer], ffn_norm[layer], moe_router[idx] if is_moe else None)
        if is_moe:
            x2, h, route = res
            x2 = _moe(h, x2, route, moe_w1[idx], moe_w3[idx], moe_w2[idx])
        else:
            x2, h = res
            x2 = _ffn_dense(h, x2, ffn_w1[idx], ffn_w3[idx], ffn_w2[idx])
    return x2.reshape(b, s, d).astype(x.dtype)
```

```python
import functools
import math

import jax
import jax.numpy as jnp
from jax import lax
from jax.experimental import pallas as pl
from jax.experimental.pallas import tpu as pltpu

F32 = jnp.float32
BF16 = jnp.bfloat16
I32 = jnp.int32
HIGHEST = lax.Precision.HIGHEST
NEG_INF = float("-inf")

LANES = 128
SUBLANES = 8
VMEM_BYTES_V7X = 64 * 1024 * 1024
VMEM_LIMIT = 48 * 1024 * 1024

HEAD_DIM = 64
N_NSA_HEADS = 4
N_SSD_HEADS = 8
N_FOX_HEADS = 4
D_NSA = N_NSA_HEADS * HEAD_DIM
D_SSD = N_SSD_HEADS * HEAD_DIM
D_FOX = N_FOX_HEADS * HEAD_DIM
NSA_N_KV_TENSORS = 6
NSA_CMP_BLOCK = 32
NSA_CMP_STRIDE = 16
NSA_SLC_BLOCK = 64
NSA_TOP_N = 16
NSA_WINDOW = 512
SSD_GROUPS = 2
SSD_D_STATE = 64
SSD_CONV = 4
SSD_CONV_DIM = D_SSD + 2 * SSD_GROUPS * SSD_D_STATE
N_EXPERTS = 8
TOP_K = 2
RMS_EPS = 1e-6
IN_SIZES = (D_NSA, NSA_N_KV_TENSORS * HEAD_DIM, 3 * N_NSA_HEADS,
            D_SSD, SSD_CONV_DIM, N_SSD_HEADS,
            D_FOX, D_FOX, D_FOX, N_FOX_HEADS)

C_QN, W_QN = 0, N_NSA_HEADS * LANES
C_KVN, W_KVN = C_QN + W_QN, NSA_N_KV_TENSORS * HEAD_DIM
C_Z, W_Z = C_KVN + W_KVN, D_SSD
C_XBC, W_XBC = C_Z + W_Z, SSD_CONV_DIM
C_QF, W_QF = C_XBC + W_XBC, N_FOX_HEADS * LANES
C_KF, W_KF = C_QF + W_QF, D_FOX
C_VF, W_VF = C_KF + W_KF, D_FOX
C_SM, W_SM = C_VF + W_VF, LANES
D_INP = C_SM + W_SM
SM_DT, SM_F, SM_G = 0, N_SSD_HEADS, N_SSD_HEADS + N_FOX_HEADS

TM_PROJ = 512
SSD_L = 128
FOX_TQ = 512
NSA_TQ = 128
NSA_TK = 512
FFN_TM = 1024
FFN_TF = 256
MOE_TM = 512
MOE_TT = 256


def _cparams(sem):
    return pltpu.CompilerParams(dimension_semantics=sem, vmem_limit_bytes=VMEM_LIMIT)


def _dot(a, b, **kw):
    return jnp.dot(a, b, preferred_element_type=F32, **kw)


def _dot_nt(a, b, **kw):
    return lax.dot_general(a, b, (((1,), (1,)), ((), ())), preferred_element_type=F32, **kw)


def _inproj_body(x_ref, g_ref, w_ref, gain_ref, sbias_ref,
                 qn_ref, kvn_ref, z_ref, xbc_ref, qf_ref, kf_ref, vf_ref, sm_ref):
    x = x_ref[...]
    ms = jnp.mean(x * x, axis=-1, keepdims=True)
    h = (x * lax.rsqrt(ms + RMS_EPS) * g_ref[...]).astype(BF16)
    lane = lax.broadcasted_iota(I32, (x.shape[0], LANES), 1)
    lo = lane < HEAD_DIM

    def proj(c0, wd):
        return _dot(h, w_ref[:, c0:c0 + wd])

    def gnorm(v, c0, modes):
        outs = []
        for s, mode in enumerate(modes):
            sl = v[:, LANES * s:LANES * (s + 1)]
            if mode is None:
                outs.append(sl)
                continue
            gain = gain_ref[:, c0 + LANES * s:c0 + LANES * (s + 1)]
            sq = sl * sl
            s_lo = jnp.sum(jnp.where(lo, sq, 0.0), axis=-1, keepdims=True)
            if mode == "both":
                s_hi = jnp.sum(jnp.where(lo, 0.0, sq), axis=-1, keepdims=True)
                msq = jnp.where(lo, s_lo, s_hi) * (1.0 / HEAD_DIM)
                r = lax.rsqrt(msq + RMS_EPS) * gain
            else:
                r = jnp.where(lo, lax.rsqrt(s_lo * (1.0 / HEAD_DIM) + RMS_EPS) * gain, 1.0)
            outs.append(sl * r)
        return jnp.concatenate(outs, axis=1)

    qn_ref[...] = gnorm(proj(C_QN, W_QN), C_QN, ["both"] * N_NSA_HEADS).astype(BF16)
    kvn_ref[...] = gnorm(proj(C_KVN, W_KVN), C_KVN, [None, "lo", "lo"]).astype(BF16)
    z_ref[...] = proj(C_Z, W_Z).astype(BF16)
    xbc_ref[...] = proj(C_XBC, W_XBC)
    qf_ref[...] = gnorm(proj(C_QF, W_QF), C_QF, ["both"] * N_FOX_HEADS).astype(BF16)
    kf_ref[...] = gnorm(proj(C_KF, W_KF), C_KF, ["both"] * (W_KF // LANES)).astype(BF16)
    vf_ref[...] = proj(C_VF, W_VF).astype(BF16)

    v = proj(C_SM, W_SM) + sbias_ref[...]
    e = jnp.exp(-jnp.abs(v))
    l1p = jnp.log1p(e)
    softplus = jnp.maximum(v, 0.0) + l1p
    log_sig = jnp.minimum(v, 0.0) - l1p
    sig = jnp.where(v >= 0.0, 1.0 / (1.0 + e), e / (1.0 + e))
    sm_ref[...] = jnp.where(lane < SM_F, softplus,
                            jnp.where(lane < SM_G, log_sig,
                                      jnp.where(lane < SM_G + 3 * N_NSA_HEADS, sig, 0.0)))


def _inproj(x2, g, w, gains, sbias):
    n, d = x2.shape
    tm = min(TM_PROJ, n)
    row = lambda wd: pl.BlockSpec((tm, wd), lambda i: (i, 0))
    full = lambda a: pl.BlockSpec(a.shape, lambda i: (0, 0))
    outs = [(W_QN, BF16), (W_KVN, BF16), (W_Z, BF16), (W_XBC, F32),
            (W_QF, BF16), (W_KF, BF16), (W_VF, BF16), (W_SM, F32)]
    return pl.pallas_call(
        _inproj_body,
        grid=(n // tm,),
        in_specs=[row(d), full(g), full(w), full(gains), full(sbias)],
        out_specs=[row(wd) for wd, _ in outs],
        out_shape=[jax.ShapeDtypeStruct((n, wd), dt) for wd, dt in outs],
        compiler_params=_cparams(("parallel",)),
        name="inproj",
    )(x2, g, w, gains, sbias)


def _pack_in_weights(w_in, nsa_q_norm, nsa_k_norm, fox_q_norm, fox_k_norm, dt_bias, f_bias):
    d = w_in.shape[0]
    pts = []
    acc = 0
    for s in IN_SIZES[:-1]:
        acc += s
        pts.append(acc)
    q_n, kv_n, g_n, z, xbc, dt, q_f, k_f, v_f, f_f = jnp.split(w_in, pts, axis=1)
    zw = jnp.zeros((d, HEAD_DIM), w_in.dtype)
    zg = jnp.zeros((HEAD_DIM,), F32)
    one = jnp.ones((HEAD_DIM,), F32)
    scale = HEAD_DIM ** -0.5
    cols, gains = [], []
    for h in range(N_NSA_HEADS):
        cols += [q_n[:, HEAD_DIM * h:HEAD_DIM * (h + 1)], zw]
        gains += [nsa_q_norm * scale, zg]
    cols.append(kv_n)
    gains += [one, one, nsa_k_norm, one, nsa_k_norm, one]
    cols += [z, xbc]
    gains += [jnp.ones((W_Z + W_XBC,), F32)]
    for h in range(N_FOX_HEADS):
        qh = q_f[:, HEAD_DIM * h:HEAD_DIM * (h + 1)]
        cols += [qh, zw] if h % 2 == 0 else [zw, qh]
        gains += [fox_q_norm * scale, zg] if h % 2 == 0 else [zg, fox_q_norm * scale]
    cols += [k_f, v_f]
    gains += [jnp.tile(fox_k_norm, N_FOX_HEADS), jnp.ones((W_VF,), F32)]
    pad = W_SM - (N_SSD_HEADS + N_FOX_HEADS + 3 * N_NSA_HEADS)
    cols += [dt, f_f, g_n, jnp.zeros((d, pad), w_in.dtype)]
    gains += [jnp.ones((W_SM,), F32)]
    w = jnp.concatenate(cols, axis=1).astype(BF16)
    gain = jnp.concatenate([g.astype(F32) for g in gains]).reshape(1, D_INP)
    sbias = jnp.concatenate([dt_bias.astype(F32), f_bias.astype(F32),
                             jnp.zeros((W_SM - N_SSD_HEADS - N_FOX_HEADS,), F32)]).reshape(1, W_SM)
    return w, gain, sbias


def _ssd_body(xbc_ref, z_ref, sm_ref, cw_ref, alog_ref, dexp_ref, ng_ref, e_ref, bm_ref,
              o_ref, ccol_ref, crow_ref, xw_ref, st_ref, fc_ref):
    L = xbc_ref.shape[1]
    c = pl.program_id(1)

    @pl.when(c == 0)
    def _():
        xw_ref[0:SUBLANES, :] = jnp.zeros((SUBLANES, W_XBC), F32)
        st_ref[...] = jnp.zeros_like(st_ref)
        fc_ref[...] = jnp.zeros_like(fc_ref)

    xw_ref[SUBLANES:SUBLANES + L, :] = xbc_ref[0]
    acc = jnp.broadcast_to(cw_ref[SSD_CONV:SSD_CONV + 1, :], (L, W_XBC))
    for k in range(SSD_CONV):
        r0 = SUBLANES - (SSD_CONV - 1) + k
        acc = acc + cw_ref[k:k + 1, :] * xw_ref[r0:r0 + L, :]
    xw_ref[0:SUBLANES, :] = xw_ref[L:L + SUBLANES, :]
    u = acc * (1.0 / (1.0 + jnp.exp(-acc)))
    xs = u[:, 0:D_SSD]
    bmat = u[:, D_SSD:D_SSD + LANES]
    cmat = u[:, D_SSD + LANES:D_SSD + 2 * LANES]

    lane = lax.broadcasted_iota(I32, (L, LANES), 1)
    sm = sm_ref[0]
    a_row = jnp.where(lane[0:1] < N_SSD_HEADS, -jnp.exp(alog_ref[...]), 0.0)
    is_dt = lane < SM_F
    is_f = (lane >= SM_F) & (lane < SM_G)
    vals = jnp.where(is_dt, sm * a_row, jnp.where(is_f, sm, 0.0))
    ri = lax.broadcasted_iota(I32, (L, L), 0)
    ci = lax.broadcasted_iota(I32, (L, L), 1)
    causal = ci <= ri
    tril = jnp.where(causal, 1.0, 0.0).astype(F32)
    cs_all = _dot(tril, vals, precision=HIGHEST)

    ccol = jnp.where(is_f, cs_all + fc_ref[...], 0.0)
    fc_ref[...] = ccol[L - 1:L, :]
    ccol_ref[0] = ccol
    crow_ref[0] = ccol.T[SM_F:SM_F + SUBLANES, :]

    cs = jnp.where(is_dt, cs_all, 0.0)
    cs_t = cs.T
    cs_exp = _dot(cs, e_ref[...], precision=HIGHEST)
    dt_exp = _dot(jnp.where(is_dt, sm, 0.0), e_ref[...], precision=HIGHEST)
    xdt = xs * dt_exp
    decay_out = jnp.exp(cs_exp)
    cs_last = cs_exp[L - 1:L, :]
    dte = jnp.exp(cs_last - cs_exp)
    chunk_decay = jnp.exp(cs_last)

    lo = lane < HEAD_DIM
    b_bf = bmat.astype(BF16)
    xdt_bf = xdt.astype(BF16)
    c_bf = cmat.astype(BF16)
    sc = [_dot_nt(jnp.where(lo, cmat, 0.0).astype(BF16), b_bf),
          _dot_nt(jnp.where(lo, 0.0, cmat).astype(BF16), b_bf)]
    y_pairs = []
    for pr in range(N_SSD_HEADS // 2):
        ys = []
        for hh in range(2):
            h = 2 * pr + hh
            grp = h // (N_SSD_HEADS // SSD_GROUPS)
            seg = cs[:, h:h + 1] - cs_t[h:h + 1, :]
            lmat = jnp.exp(jnp.where(causal, seg, NEG_INF))
            m = (sc[grp] * lmat).astype(BF16)
            ys.append(_dot(m, xdt_bf[:, LANES * pr:LANES * (pr + 1)]))
        y_pairs.append(jnp.where(lo, ys[0], ys[1]))
    y_diag = jnp.concatenate(y_pairs, axis=1)

    state = st_ref[...]
    y_off = _dot(c_bf, state.astype(BF16)) * decay_out
    bt = bmat.T.astype(BF16)
    st_new = _dot(bt, (xdt * dte).astype(BF16))
    st_ref[...] = state * chunk_decay + bm_ref[...] * st_new

    y = y_diag + y_off + xs * dexp_ref[...]
    zf = z_ref[0].astype(F32)
    y = y * (zf * (1.0 / (1.0 + jnp.exp(-zf))))
    ms = jnp.mean(y * y, axis=-1, keepdims=True)
    o_ref[0] = (y * lax.rsqrt(ms + RMS_EPS) * ng_ref[...]).astype(BF16)


def _ssd(xbc, z, sm, conv_w, conv_b, a_log, d_skip, norm_g):
    b, s, _ = xbc.shape
    L = min(SSD_L, s)
    cw = jnp.concatenate([conv_w.astype(F32), conv_b.astype(F32)[None, :],
                          jnp.zeros((SUBLANES - SSD_CONV - 1, W_XBC), F32)], axis=0)
    alog = jnp.concatenate([a_log.astype(F32), jnp.zeros((LANES - N_SSD_HEADS,), F32)]).reshape(1, LANES)
    dexp = jnp.repeat(d_skip.astype(F32), HEAD_DIM).reshape(1, D_SSD)
    ng = norm_g.astype(F32).reshape(1, D_SSD)
    hid = jnp.arange(LANES)[:, None]
    col = jnp.arange(D_SSD)[None, :]
    emat = ((col // HEAD_DIM) == hid).astype(F32)
    heads_per_group = N_SSD_HEADS // SSD_GROUPS
    bmask = ((hid // SSD_D_STATE) == (col // (HEAD_DIM * heads_per_group))).astype(F32)
    blk = lambda wd: pl.BlockSpec((1, L, wd), lambda bi, ci: (bi, ci, 0))
    full = lambda a: pl.BlockSpec(a.shape, lambda bi, ci: (0, 0))
    return pl.pallas_call(
        _ssd_body,
        grid=(b, s // L),
        in_specs=[blk(W_XBC), blk(D_SSD), blk(LANES), full(cw), full(alog), full(dexp), full(ng),
                  full(emat), full(bmask)],
        out_specs=[blk(D_SSD), blk(LANES), pl.BlockSpec((1, SUBLANES, L), lambda bi, ci: (bi, 0, ci))],
        out_shape=[jax.ShapeDtypeStruct((b, s, D_SSD), BF16),
                   jax.ShapeDtypeStruct((b, s, LANES), F32),
                   jax.ShapeDtypeStruct((b, SUBLANES, s), F32)],
        scratch_shapes=[pltpu.VMEM((L + SUBLANES, W_XBC), F32),
                        pltpu.VMEM((LANES, D_SSD), F32),
                        pltpu.VMEM((1, LANES), F32)],
        compiler_params=_cparams(("parallel", "arbitrary")),
        name="ssd_scan",
    )(xbc, z, sm, cw, alog, dexp, ng, emat, bmask)


def _fox_body(q_ref, k_ref, v_ref, ccol_ref, crow_ref, o_ref, m_ref, l_ref, acc_ref):
    tq = q_ref.shape[1]
    tk = tq
    qi = pl.program_id(1)
    m_ref[...] = jnp.full_like(m_ref, NEG_INF)
    l_ref[...] = jnp.zeros_like(l_ref)
    acc_ref[...] = jnp.zeros_like(acc_ref)
    ccol = ccol_ref[0]
    ri = lax.broadcasted_iota(I32, (tq, tk), 0)
    ci = lax.broadcasted_iota(I32, (tq, tk), 1)
    diag_mask = ci <= ri

    def tile(j, masked):
        k0 = pl.multiple_of(j * tk, tk)
        for pr in range(N_FOX_HEADS // 2):
            kslab = k_ref[0, pl.ds(k0, tk), LANES * pr:LANES * (pr + 1)]
            vslab = v_ref[0, pl.ds(k0, tk), LANES * pr:LANES * (pr + 1)]
            for hh in range(2):
                h = 2 * pr + hh
                q = q_ref[0, :, LANES * h:LANES * (h + 1)]
                s = _dot_nt(q, kslab)
                s = s + (ccol[:, SM_F + h:SM_F + h + 1] - crow_ref[0, h:h + 1, pl.ds(k0, tk)])
                if masked:
                    s = jnp.where(diag_mask, s, NEG_INF)
                m_prev = m_ref[h]
                m_new = jnp.maximum(m_prev, jnp.max(s, axis=-1, keepdims=True))
                alpha = jnp.exp(m_prev - m_new)
                p = jnp.exp(s - m_new)
                l_ref[h] = alpha * l_ref[h] + jnp.sum(p, axis=-1, keepdims=True)
                acc_ref[h] = alpha * acc_ref[h] + _dot(p.astype(BF16), vslab)
                m_ref[h] = m_new

    def body(j, carry):
        tile(j, False)
        return carry

    lax.fori_loop(0, qi, body, 0)
    tile(qi, True)

    lane = lax.broadcasted_iota(I32, (tq, LANES), 1)
    lo = lane < HEAD_DIM
    outs = []
    for pr in range(N_FOX_HEADS // 2):
        tiny = jnp.finfo(F32).tiny
        o0 = acc_ref[2 * pr] / jnp.maximum(l_ref[2 * pr], tiny)
        o1 = acc_ref[2 * pr + 1] / jnp.maximum(l_ref[2 * pr + 1], tiny)
        outs.append(jnp.where(lo, o0, o1))
    o_ref[0] = jnp.concatenate(outs, axis=1).astype(BF16)


def _fox(qf, kf, vf, ccol, crow):
    b, s, _ = qf.shape
    tq = min(FOX_TQ, s)
    return pl.pallas_call(
        _fox_body,
        grid=(b, s // tq),
        in_specs=[pl.BlockSpec((1, tq, W_QF), lambda bi, qi: (bi, qi, 0)),
                  pl.BlockSpec((1, s, W_KF), lambda bi, qi: (bi, 0, 0)),
                  pl.BlockSpec((1, s, W_VF), lambda bi, qi: (bi, 0, 0)),
                  pl.BlockSpec((1, tq, LANES), lambda bi, qi: (bi, qi, 0)),
                  pl.BlockSpec((1, SUBLANES, s), lambda bi, qi: (bi, 0, 0))],
        out_specs=pl.BlockSpec((1, tq, D_FOX), lambda bi, qi: (bi, qi, 0)),
        out_shape=jax.ShapeDtypeStruct((b, s, D_FOX), BF16),
        scratch_shapes=[pltpu.VMEM((N_FOX_HEADS, tq, 1), F32),
                        pltpu.VMEM((N_FOX_HEADS, tq, 1), F32),
                        pltpu.VMEM((N_FOX_HEADS, tq, LANES), F32)],
        compiler_params=_cparams(("parallel", "arbitrary")),
        name="fox_attn",
    )(qf, kf, vf, ccol, crow)


def _cmp_body(gk_ref, gv_ref, wkt_ref, wkb_ref, wvt_ref, wvb_ref, pe_ref, gain_ref, o_ref):
    ncp = gk_ref.shape[1]
    gk = gk_ref[0]
    gv = gv_ref[0]
    top = _dot(gk, wkt_ref[...]) + _dot(gv, wvt_ref[...])
    bot = _dot(gk, wkb_ref[...]) + _dot(gv, wvb_ref[...])
    bot_next = pltpu.roll(bot, ncp - 1, 0)
    raw = top + bot_next + pe_ref[0:1, :]
    lane = lax.broadcasted_iota(I32, raw.shape, 1)
    lo = lane < HEAD_DIM
    msq = jnp.sum(jnp.where(lo, raw * raw, 0.0), axis=-1, keepdims=True) * (1.0 / HEAD_DIM)
    out = jnp.where(lo, raw * lax.rsqrt(msq + RMS_EPS) * gain_ref[...], raw)
    row = lax.broadcasted_iota(I32, raw.shape, 0)
    o_ref[0] = jnp.where(row < ncp - 1, out, 0.0).astype(BF16)


def _nsa_compress(kvn, pe_k, pe_v, w_ck, w_cv, k_norm):
    b, s, _ = kvn.shape
    half = NSA_CMP_STRIDE * HEAD_DIM
    ng = s // NSA_CMP_STRIDE
    gk = kvn[:, :, 0:HEAD_DIM].reshape(b, ng, half)
    gv = kvn[:, :, HEAD_DIM:2 * HEAD_DIM].reshape(b, ng, half)
    zc = jnp.zeros((half, HEAD_DIM), F32)
    wk = w_ck.astype(F32)
    wv = w_cv.astype(F32)
    wkt = jnp.concatenate([wk[:half], zc], axis=1).astype(BF16)
    wkb = jnp.concatenate([wk[half:], zc], axis=1).astype(BF16)
    wvt = jnp.concatenate([zc, wv[:half]], axis=1).astype(BF16)
    wvb = jnp.concatenate([zc, wv[half:]], axis=1).astype(BF16)
    pe_const = jnp.concatenate([jnp.dot(pe_k.reshape(1, -1).astype(F32), wk, precision=HIGHEST),
                                jnp.dot(pe_v.reshape(1, -1).astype(F32), wv, precision=HIGHEST)], axis=1)
    pe_const = jnp.concatenate([pe_const, jnp.zeros((SUBLANES - 1, LANES), F32)], axis=0)
    gain = jnp.concatenate([k_norm.astype(F32), jnp.ones((HEAD_DIM,), F32)]).reshape(1, LANES)
    blk = pl.BlockSpec((1, ng, half), lambda bi: (bi, 0, 0))
    full = lambda a: pl.BlockSpec(a.shape, lambda bi: (0, 0))
    return pl.pallas_call(
        _cmp_body,
        grid=(b,),
        in_specs=[blk, blk, full(wkt), full(wkb), full(wvt), full(wvb), full(pe_const), full(gain)],
        out_specs=pl.BlockSpec((1, ng, LANES), lambda bi: (bi, 0, 0)),
        out_shape=jax.ShapeDtypeStruct((b, ng, LANES), BF16),
        compiler_params=_cparams(("parallel",)),
        name="nsa_compress",
    )(gk, gv, wkt, wkb, wvt, wvb, pe_const, gain)


def _nsa_body(q_ref, kv_ref, kc_ref, sm_ref, ovt_ref, e_ref, o_ref,
              qs_ref, sc_ref, m_ref, l_ref, acc_ref, *, n_slc, top_n):
    tq = q_ref.shape[1]
    tk = e_ref.shape[2]
    ncp = kc_ref.shape[1]
    nh = N_NSA_HEADS
    rows = nh * tq
    qi = pl.program_id(1)
    t0 = qi * tq
    tiny = jnp.finfo(F32).tiny

    for h in range(nh):
        qs_ref[h * tq:(h + 1) * tq, :] = q_ref[0, :, LANES * h:LANES * (h + 1)]
    qs = qs_ref[...]

    kc = kc_ref[0]
    s_c = _dot_nt(qs, kc).reshape(nh, tq, ncp)
    t_q = t0 + lax.broadcasted_iota(I32, (1, tq, ncp), 1)
    c_i = lax.broadcasted_iota(I32, (1, tq, ncp), 2)
    valid_c = (c_i * NSA_CMP_STRIDE + (NSA_CMP_BLOCK - 1) <= t_q) & (c_i < ncp - 1)
    s_c = jnp.where(valid_c, s_c, NEG_INF)
    m_c = jnp.max(s_c, axis=-1, keepdims=True)
    e_c = jnp.exp(s_c - jnp.where(m_c == NEG_INF, 0.0, m_c))
    p_c = e_c / jnp.maximum(jnp.sum(e_c, axis=-1, keepdims=True), tiny)
    o_c = _dot(p_c.reshape(rows, ncp).astype(BF16), kc)

    p_sum = jnp.sum(p_c, axis=0)
    imp_t = _dot_nt(ovt_ref[...], p_sum, precision=HIGHEST)
    n_i = lax.broadcasted_iota(I32, (LANES, tq), 0)
    t_l = t0 + lax.broadcasted_iota(I32, (LANES, tq), 1)
    avail = n_i * NSA_SLC_BLOCK <= t_l
    forced = (n_i * NSA_SLC_BLOCK <= t_l) & (t_l < (n_i + 1) * NSA_SLC_BLOCK) | (n_i == 0)
    score = jnp.where(avail, jnp.where(forced, jnp.inf, imp_t), NEG_INF)
    sc_ref[...] = score
    ns_pad = ((n_slc + SUBLANES - 1) // SUBLANES) * SUBLANES
    sc = sc_ref[0:ns_pad, :]
    n_s = n_i[0:ns_pad, :]
    cnt = jnp.zeros((ns_pad, tq), F32)
    for mth in range(n_slc):
        r = sc_ref[mth:mth + 1, :]
        cnt = cnt + jnp.where(r > sc, 1.0, jnp.where((r == sc) & (n_s > mth), 1.0, 0.0))
    sel_t = jnp.where(avail[0:ns_pad, :] & (cnt < float(top_n)), 1.0, 0.0)
    sc_ref[0:ns_pad, :] = sel_t
    if ns_pad < LANES:
        sc_ref[ns_pad:LANES, :] = jnp.zeros((LANES - ns_pad, tq), F32)
    sel = sc_ref[...].T.astype(BF16)

    t_row = t0 + lax.broadcasted_iota(I32, (1, tq, tk), 1)
    k_col = lax.broadcasted_iota(I32, (1, tq, tk), 2)

    def attend(br, j, slab, mask):
        k0 = pl.multiple_of(j * tk, tk)
        kv = kv_ref[0, pl.ds(k0, tk), LANES * slab:LANES * (slab + 1)]
        s = _dot_nt(qs, kv).reshape(nh, tq, tk) + jnp.where(mask, 0.0, NEG_INF)
        m_prev = m_ref[br]
        m_new = jnp.maximum(m_prev, jnp.max(s, axis=-1, keepdims=True))
        m_safe = jnp.where(m_new == NEG_INF, 0.0, m_new)
        alpha = jnp.exp(m_prev - m_safe)
        p = jnp.exp(s - m_safe)
        l_ref[br] = alpha * l_ref[br] + jnp.sum(p, axis=-1, keepdims=True)
        pv = _dot(p.reshape(rows, tk).astype(BF16), kv).reshape(nh, tq, LANES)
        acc_ref[br] = alpha * acc_ref[br] + pv
        m_ref[br] = m_new

    m_ref[...] = jnp.full_like(m_ref, NEG_INF)
    l_ref[...] = jnp.zeros_like(l_ref)
    acc_ref[...] = jnp.zeros_like(acc_ref)

    def sel_body(j, carry):
        pos = j * tk + k_col
        blk_mask = _dot(sel, e_ref[j]) > 0.5
        attend(0, j, 1, blk_mask[None] & (pos <= t_row))
        return carry

    lax.fori_loop(0, (t0 + tq + tk - 1) // tk, sel_body, 0)

    def win_body(j, carry):
        pos = j * tk + k_col
        attend(1, j, 2, (pos <= t_row) & (pos > t_row - NSA_WINDOW))
        return carry

    w_lo = jnp.maximum(t0 - (NSA_WINDOW - 1), 0) // tk
    lax.fori_loop(w_lo, (t0 + tq + tk - 1) // tk, win_body, 0)

    sm = sm_ref[0]
    o_c3 = o_c.reshape(nh, tq, LANES)
    lane = lax.broadcasted_iota(I32, (tq, LANES), 1)
    lo = lane < HEAD_DIM
    heads = []
    for h in range(nh):
        g0 = sm[:, SM_G + 3 * h:SM_G + 3 * h + 1]
        g1 = sm[:, SM_G + 3 * h + 1:SM_G + 3 * h + 2]
        g2 = sm[:, SM_G + 3 * h + 2:SM_G + 3 * h + 3]
        o_s = acc_ref[0, h] / jnp.maximum(l_ref[0, h], tiny)
        o_w = acc_ref[1, h] / jnp.maximum(l_ref[1, h], tiny)
        heads.append(g0 * o_c3[h] + g1 * o_s + g2 * o_w)
    outs = []
    for pr in range(nh // 2):
        outs.append(jnp.where(lo, pltpu.roll(heads[2 * pr], HEAD_DIM, 1), heads[2 * pr + 1]))
    o_ref[0] = jnp.concatenate(outs, axis=1).astype(BF16)


def _nsa(qn, kvn, kc, sm):
    b, s, _ = qn.shape
    tq = min(NSA_TQ, s)
    tk = min(NSA_TK, s)
    n_cmp = (s - NSA_CMP_BLOCK) // NSA_CMP_STRIDE + 1
    n_slc = s // NSA_SLC_BLOCK
    ncp = kc.shape[1]
    assert n_cmp == ncp - 1 and n_slc <= LANES
    top_n = min(NSA_TOP_N, n_slc)
    c_start = jnp.arange(ncp) * NSA_CMP_STRIDE
    n_start = jnp.arange(LANES) * NSA_SLC_BLOCK
    ovt = ((c_start[None, :] < n_start[:, None] + NSA_SLC_BLOCK)
           & (c_start[None, :] + NSA_CMP_BLOCK > n_start[:, None])
           & (jnp.arange(ncp)[None, :] < n_cmp) & (jnp.arange(LANES)[:, None] < n_slc)).astype(F32)
    pos = jnp.arange(s).reshape(s // tk, 1, tk)
    emat = ((pos // NSA_SLC_BLOCK) == jnp.arange(LANES)[None, :, None]).astype(BF16)
    rows = N_NSA_HEADS * tq
    return pl.pallas_call(
        functools.partial(_nsa_body, n_slc=n_slc, top_n=top_n),
        grid=(b, s // tq),
        in_specs=[pl.BlockSpec((1, tq, W_QN), lambda bi, qi: (bi, qi, 0)),
                  pl.BlockSpec((1, s, W_KVN), lambda bi, qi: (bi, 0, 0)),
                  pl.BlockSpec((1, ncp, LANES), lambda bi, qi: (bi, 0, 0)),
                  pl.BlockSpec((1, tq, LANES), lambda bi, qi: (bi, qi, 0)),
                  pl.BlockSpec(ovt.shape, lambda bi, qi: (0, 0)),
                  pl.BlockSpec(emat.shape, lambda bi, qi: (0, 0, 0))],
        out_specs=pl.BlockSpec((1, tq, D_NSA), lambda bi, qi: (bi, qi, 0)),
        out_shape=jax.ShapeDtypeStruct((b, s, D_NSA), BF16),
        scratch_shapes=[pltpu.VMEM((rows, LANES), BF16),
                        pltpu.VMEM((LANES, tq), F32),
                        pltpu.VMEM((2, N_NSA_HEADS, tq, 1), F32),
                        pltpu.VMEM((2, N_NSA_HEADS, tq, 1), F32),
                        pltpu.VMEM((2, N_NSA_HEADS, tq, LANES), F32)],
        compiler_params=_cparams(("parallel", "arbitrary")),
        name="nsa_attn",
    )(qn, kvn, kc, sm, ovt, emat)


def _outproj_body(*refs, route):
    if route:
        (on_ref, os_ref, of_ref, x_ref, wn_ref, ws_ref, wf_ref, g_ref, wr_ref,
         xo_ref, h_ref, rt_ref) = refs
    else:
        on_ref, os_ref, of_ref, x_ref, wn_ref, ws_ref, wf_ref, g_ref, xo_ref, h_ref = refs
    x = (x_ref[...] + _dot(on_ref[...], wn_ref[...]) + _dot(os_ref[...], ws_ref[...])
         + _dot(of_ref[...], wf_ref[...]))
    xo_ref[...] = x
    ms = jnp.mean(x * x, axis=-1, keepdims=True)
    h = x * lax.rsqrt(ms + RMS_EPS) * g_ref[...]
    h_ref[...] = h.astype(h_ref.dtype)
    if route:
        logits = _dot(h, wr_ref[...], precision=HIGHEST)
        lane = lax.broadcasted_iota(I32, logits.shape, 1)
        lf = lane.astype(F32)
        lg = jnp.where(lane < N_EXPERTS, logits, NEG_INF)
        m1 = jnp.max(lg, axis=-1, keepdims=True)
        i1 = jnp.min(jnp.where(lg == m1, lf, float(LANES)), axis=-1, keepdims=True)
        lg2 = jnp.where(lf == i1, NEG_INF, lg)
        m2 = jnp.max(lg2, axis=-1, keepdims=True)
        i2 = jnp.min(jnp.where(lg2 == m2, lf, float(LANES)), axis=-1, keepdims=True)
        e21 = jnp.exp(m2 - m1)
        g1 = 1.0 / (1.0 + e21)
        g2 = e21 / (1.0 + e21)
        rt_ref[...] = jnp.where(lane == 0, i1, jnp.where(lane == 1, i2,
                                jnp.where(lane == 2, g1, jnp.where(lane == 3, g2, 0.0))))


def _outproj(o_nsa, o_ssd, o_fox, x2, w_out, g, w_router=None):
    n, d = x2.shape
    tm = min(TM_PROJ, n)
    route = w_router is not None
    wn = w_out[0:D_NSA].astype(BF16)
    ws = w_out[D_NSA:D_NSA + D_SSD].astype(BF16)
    wf = w_out[D_NSA + D_SSD:].astype(BF16)
    gg = g.astype(F32).reshape(1, d)
    row = lambda wd: pl.BlockSpec((tm, wd), lambda i: (i, 0))
    full = lambda a: pl.BlockSpec(a.shape, lambda i: (0, 0))
    ins = [o_nsa, o_ssd, o_fox, x2, wn, ws, wf, gg]
    in_specs = [row(D_NSA), row(D_SSD), row(D_FOX), row(d), full(wn), full(ws), full(wf), full(gg)]
    out_specs = [row(d), row(d)]
    out_shape = [jax.ShapeDtypeStruct((n, d), F32),
                 jax.ShapeDtypeStruct((n, d), F32 if route else BF16)]
    if route:
        wr = jnp.concatenate([w_router.astype(F32), jnp.zeros((d, LANES - N_EXPERTS), F32)], axis=1)
        ins.append(wr)
        in_specs.append(full(wr))
        out_specs.append(row(LANES))
        out_shape.append(jax.ShapeDtypeStruct((n, LANES), F32))
    return pl.pallas_call(
        functools.partial(_outproj_body, route=route),
        grid=(n // tm,),
        in_specs=in_specs, out_specs=out_specs, out_shape=out_shape,
        compiler_params=_cparams(("parallel",)),
        name="outproj_route" if route else "outproj",
    )(*ins)


def _swiglu_acc(h, w1_ref, w3_ref, w2_ref, acc_ref, f):
    @pl.when(f == 0)
    def _():
        acc_ref[...] = jnp.zeros_like(acc_ref)

    a = _dot(h, w1_ref[...])
    b = _dot(h, w3_ref[...])
    mid = (a * (1.0 / (1.0 + jnp.exp(-a))) * b).astype(BF16)
    acc_ref[...] += _dot(mid, w2_ref[...])


def _ffn_body(h_ref, x_ref, w1_ref, w3_ref, w2_ref, o_ref, acc_ref):
    f = pl.program_id(1)
    _swiglu_acc(h_ref[...], w1_ref, w3_ref, w2_ref, acc_ref, f)

    @pl.when(f == pl.num_programs(1) - 1)
    def _():
        o_ref[...] = x_ref[...] + acc_ref[...]


def _ffn_dense(h, x2, w1, w3, w2):
    n, d = x2.shape
    ff = w1.shape[1]
    tm = min(FFN_TM, n)
    tf = FFN_TF
    return pl.pallas_call(
        _ffn_body,
        grid=(n // tm, ff // tf),
        in_specs=[pl.BlockSpec((tm, d), lambda i, f: (i, 0)),
                  pl.BlockSpec((tm, d), lambda i, f: (i, 0)),
                  pl.BlockSpec((d, tf), lambda i, f: (0, f)),
                  pl.BlockSpec((d, tf), lambda i, f: (0, f)),
                  pl.BlockSpec((tf, d), lambda i, f: (f, 0))],
        out_specs=pl.BlockSpec((tm, d), lambda i, f: (i, 0)),
        out_shape=jax.ShapeDtypeStruct((n, d), F32),
        scratch_shapes=[pltpu.VMEM((tm, d), F32)],
        compiler_params=_cparams(("parallel", "arbitrary")),
        name="ffn_dense",
    )(h, x2, w1.astype(BF16), w3.astype(BF16), w2.astype(BF16))


def _gffn_body(be_ref, xs_ref, w1_ref, w3_ref, w2_ref, o_ref, acc_ref):
    f = pl.program_id(1)
    _swiglu_acc(xs_ref[...].astype(BF16), w1_ref.at[0], w3_ref.at[0], w2_ref.at[0], acc_ref, f)

    @pl.when(f == pl.num_programs(1) - 1)
    def _():
        o_ref[...] = acc_ref[...]


def _ffn_grouped(xs, blk_e, w1, w3, w2):
    p, d = xs.shape
    ff = w1.shape[2]
    tm = MOE_TM
    tf = FFN_TF
    grid_spec = pltpu.PrefetchScalarGridSpec(
        num_scalar_prefetch=1,
        grid=(p // tm, ff // tf),
        in_specs=[pl.BlockSpec((tm, d), lambda i, f, be: (i, 0)),
                  pl.BlockSpec((1, d, tf), lambda i, f, be: (be[i], 0, f)),
                  pl.BlockSpec((1, d, tf), lambda i, f, be: (be[i], 0, f)),
                  pl.BlockSpec((1, tf, d), lambda i, f, be: (be[i], f, 0))],
        out_specs=pl.BlockSpec((tm, d), lambda i, f, be: (i, 0)),
        scratch_shapes=[pltpu.VMEM((tm, d), F32)])
    return pl.pallas_call(
        _gffn_body,
        grid_spec=grid_spec,
        out_shape=jax.ShapeDtypeStruct((p, d), F32),
        compiler_params=_cparams(("parallel", "arbitrary")),
        name="ffn_grouped",
    )(blk_e, xs, w1.astype(BF16), w3.astype(BF16), w2.astype(BF16))


def _row_copy(src, dst, sem):
    return pltpu.make_async_copy(src, dst, sem)


def _dispatch_body(tok_ref, h_hbm, o_ref, buf_ref, sem):
    tt = o_ref.shape[0]

    def issue(r, carry):
        t = tok_ref[0, 0, r]
        _row_copy(h_hbm.at[pl.ds(t, 1)], buf_ref.at[pl.ds(r, 1)], sem).start()
        return carry

    lax.fori_loop(0, tt, issue, 0)

    def drain(r, carry):
        _row_copy(h_hbm.at[pl.ds(0, 1)], buf_ref.at[pl.ds(0, 1)], sem).wait()
        return carry

    lax.fori_loop(0, tt, drain, 0)
    o_ref[...] = buf_ref[...].astype(o_ref.dtype)


def _dispatch(h, tok3):
    n, d = h.shape
    nt, _, tt = tok3.shape
    return pl.pallas_call(
        _dispatch_body,
        grid=(nt,),
        in_specs=[pl.BlockSpec((1, 1, tt), lambda i: (i, 0, 0), memory_space=pltpu.SMEM),
                  pl.BlockSpec(memory_space=pl.ANY)],
        out_specs=pl.BlockSpec((tt, d), lambda i: (i, 0)),
        out_shape=jax.ShapeDtypeStruct((nt * tt, d), BF16),
        scratch_shapes=[pltpu.VMEM((tt, d), h.dtype), pltpu.SemaphoreType.DMA(())],
        compiler_params=_cparams(("arbitrary",)),
        name="moe_dispatch",
    )(tok3, h)


def _combine_body(dest_ref, x_ref, rt_ref, y_hbm, o_ref, buf_ref, sem):
    tt = x_ref.shape[0]

    def issue(r, carry):
        for k in range(TOP_K):
            d = dest_ref[0, k, r]
            _row_copy(y_hbm.at[pl.ds(d, 1)], buf_ref.at[k, pl.ds(r, 1)], sem).start()
        return carry

    lax.fori_loop(0, tt, issue, 0)

    def drain(r, carry):
        for k in range(TOP_K):
            _row_copy(y_hbm.at[pl.ds(0, 1)], buf_ref.at[k, pl.ds(0, 1)], sem).wait()
        return carry

    lax.fori_loop(0, tt, drain, 0)
    rt = rt_ref[...]
    o_ref[...] = x_ref[...] + rt[:, 2:3] * buf_ref[0] + rt[:, 3:4] * buf_ref[1]


def _combine(x2, route, dest3, y):
    n, d = x2.shape
    tt = dest3.shape[2]
    return pl.pallas_call(
        _combine_body,
        grid=(n // tt,),
        in_specs=[pl.BlockSpec((1, TOP_K, tt), lambda i: (i, 0, 0), memory_space=pltpu.SMEM),
                  pl.BlockSpec((tt, d), lambda i: (i, 0)),
                  pl.BlockSpec((tt, LANES), lambda i: (i, 0)),
                  pl.BlockSpec(memory_space=pl.ANY)],
        out_specs=pl.BlockSpec((tt, d), lambda i: (i, 0)),
        out_shape=jax.ShapeDtypeStruct((n, d), F32),
        scratch_shapes=[pltpu.VMEM((TOP_K, tt, d), F32), pltpu.SemaphoreType.DMA(())],
        compiler_params=_cparams(("arbitrary",)),
        name="moe_combine",
    )(dest3, x2, route, y)


def _moe(h, x2, route, w1, w3, w2):
    n, d = x2.shape
    tm = MOE_TM
    tt = min(MOE_TT, n)
    n_rows = n * TOP_K
    flat_e = route[:, 0:TOP_K].astype(I32).reshape(-1)
    onehot = (flat_e[:, None] == jnp.arange(N_EXPERTS)[None, :]).astype(I32)
    csum = jnp.cumsum(onehot, axis=0)
    rank = jnp.sum(csum * onehot, axis=1) - 1
    counts = csum[-1]
    padded = ((counts + tm - 1) // tm) * tm
    pend = jnp.cumsum(padded)
    pstart = pend - padded
    dest = jnp.sum(onehot * pstart[None, :], axis=1) + rank
    n_blk = -(-n_rows // tm) + N_EXPERTS
    blk_e = jnp.minimum(jnp.searchsorted(pend, jnp.arange(n_blk) * tm, side="right"),
                        N_EXPERTS - 1).astype(I32)
    dest3 = dest.reshape(n // tt, tt, TOP_K).transpose(0, 2, 1).astype(I32)
    order = jnp.argsort(flat_e)
    start = jnp.cumsum(counts) - counts
    slot = jnp.arange(n_blk * tm)
    e_slot = blk_e[slot // tm]
    j = slot - pstart[e_slot]
    src = jnp.where(j < counts[e_slot], order[jnp.minimum(start[e_slot] + j, n_rows - 1)], 0)
    tok3 = (src // TOP_K).astype(I32).reshape(-1, 1, tt)
    xs = _dispatch(h, tok3)
    y = _ffn_grouped(xs, blk_e, w1, w3, w2)
    return _combine(x2, route, dest3, y)


def kernel(x, attn_norm, w_in, nsa_q_norm, nsa_k_norm, nsa_cmp_pe_k, nsa_cmp_pe_v, nsa_cmp_w_k,
           nsa_cmp_w_v, ssd_conv_w, ssd_conv_b, ssd_dt_bias, ssd_a_log, ssd_d, ssd_norm, fox_q_norm,
           fox_k_norm, fox_f_bias, w_out, ffn_norm, ffn_w1, ffn_w3, ffn_w2, moe_router, moe_w1,
           moe_w3, moe_w2):
    b, s, d = x.shape
    n = b * s
    depth = w_in.shape[0]
    x2 = x.reshape(n, d).astype(F32)
    for layer in range(depth):
        w, gains, sbias = _pack_in_weights(w_in[layer], nsa_q_norm[layer], nsa_k_norm[layer],
                                           fox_q_norm[layer], fox_k_norm[layer],
                                           ssd_dt_bias[layer], fox_f_bias[layer])
        g = attn_norm[layer].astype(F32).reshape(1, d)
        qn, kvn, z, xbc, qf, kf, vf, sm = _inproj(x2, g, w, gains, sbias)
        r3 = lambda a: a.reshape(b, s, a.shape[-1])
        kvn3, sm3 = r3(kvn), r3(sm)
        kc = _nsa_compress(kvn3, nsa_cmp_pe_k[layer], nsa_cmp_pe_v[layer],
                           nsa_cmp_w_k[layer], nsa_cmp_w_v[layer], nsa_k_norm[layer])
        o_nsa = _nsa(r3(qn), kvn3, kc, sm3)
        o_ssd, ccol, crow = _ssd(r3(xbc), r3(z), sm3, ssd_conv_w[layer], ssd_conv_b[layer],
                                 ssd_a_log[layer], ssd_d[layer], ssd_norm[layer])
        o_fox = _fox(r3(qf), r3(kf), r3(vf), ccol, crow)
        idx = layer // 2
        is_moe = layer % 2 == 1
        res = _outproj(o_nsa.reshape(n, D_NSA), o_ssd.reshape(n, D_SSD), o_fox.reshape(n, D_FOX),
                       x2, w_out[layer], ffn_norm[layer], moe_router[idx] if is_moe else None)
        if is_moe:
            x2, h, route = res
            x2 = _moe(h, x2, route, moe_w1[idx], moe_w3[idx], moe_w2[idx])
        else:
            x2, h = res
            x2 = _ffn_dense(h, x2, ffn_w1[idx], ffn_w3[idx], ffn_w2[idx])
    return x2.reshape(b, s, d).astype(x.dtype)
```

```python
import functools
import math

import jax
import jax.numpy as jnp
from jax import lax
from jax.experimental import pallas as pl
from jax.experimental.pallas import tpu as pltpu

F32 = jnp.float32
BF16 = jnp.bfloat16
I32 = jnp.int32
HIGHEST = lax.Precision.HIGHEST
NEG_INF = float("-inf")

LANES = 128
SUBLANES = 8
VMEM_BYTES_V7X = 64 * 1024 * 1024
VMEM_LIMIT = 48 * 1024 * 1024

HEAD_DIM = 64
N_NSA_HEADS = 4
N_SSD_HEADS = 8
N_FOX_HEADS = 4
D_NSA = N_NSA_HEADS * HEAD_DIM
D_SSD = N_SSD_HEADS * HEAD_DIM
D_FOX = N_FOX_HEADS * HEAD_DIM
NSA_N_KV_TENSORS = 6
NSA_CMP_BLOCK = 32
NSA_CMP_STRIDE = 16
NSA_SLC_BLOCK = 64
NSA_TOP_N = 16
NSA_WINDOW = 512
SSD_GROUPS = 2
SSD_D_STATE = 64
SSD_CONV = 4
SSD_CONV_DIM = D_SSD + 2 * SSD_GROUPS * SSD_D_STATE
N_EXPERTS = 8
TOP_K = 2
RMS_EPS = 1e-6
IN_SIZES = (D_NSA, NSA_N_KV_TENSORS * HEAD_DIM, 3 * N_NSA_HEADS,
            D_SSD, SSD_CONV_DIM, N_SSD_HEADS,
            D_FOX, D_FOX, D_FOX, N_FOX_HEADS)

C_QN, W_QN = 0, N_NSA_HEADS * LANES
C_KVN, W_KVN = C_QN + W_QN, NSA_N_KV_TENSORS * HEAD_DIM
C_Z, W_Z = C_KVN + W_KVN, D_SSD
C_XBC, W_XBC = C_Z + W_Z, SSD_CONV_DIM
C_QF, W_QF = C_XBC + W_XBC, N_FOX_HEADS * LANES
C_KF, W_KF = C_QF + W_QF, D_FOX
C_VF, W_VF = C_KF + W_KF, D_FOX
C_SM, W_SM = C_VF + W_VF, LANES
D_INP = C_SM + W_SM
SM_DT, SM_F, SM_G = 0, N_SSD_HEADS, N_SSD_HEADS + N_FOX_HEADS

TM_PROJ = 512
SSD_L = 128
FOX_TQ = 512
NSA_TQ = 128
NSA_TK = 512
FFN_TM = 512
FFN_TF = 1408
MOE_TM = 512
MOE_TT = 256


def _cparams(sem):
    return pltpu.CompilerParams(dimension_semantics=sem, vmem_limit_bytes=VMEM_LIMIT)


def _dot(a, b, **kw):
    return jnp.dot(a, b, preferred_element_type=F32, **kw)


def _dot_nt(a, b, **kw):
    return lax.dot_general(a, b, (((1,), (1,)), ((), ())), preferred_element_type=F32, **kw)


def _inproj_body(x_ref, g_ref, w_ref, gain_ref, sbias_ref,
                 qn_ref, kvn_ref, z_ref, xbc_ref, qf_ref, kf_ref, vf_ref, sm_ref):
    x = x_ref[...]
    ms = jnp.mean(x * x, axis=-1, keepdims=True)
    h = (x * lax.rsqrt(ms + RMS_EPS) * g_ref[...]).astype(BF16)
    lane = lax.broadcasted_iota(I32, (x.shape[0], LANES), 1)
    lo = lane < HEAD_DIM

    def proj(c0, wd):
        return _dot(h, w_ref[:, c0:c0 + wd])

    def gnorm(v, c0, modes):
        outs = []
        for s, mode in enumerate(modes):
            sl = v[:, LANES * s:LANES * (s + 1)]
            if mode is None:
                outs.append(sl)
                continue
            gain = gain_ref[:, c0 + LANES * s:c0 + LANES * (s + 1)]
            sq = sl * sl
            s_lo = jnp.sum(jnp.where(lo, sq, 0.0), axis=-1, keepdims=True)
            if mode == "both":
                s_hi = jnp.sum(jnp.where(lo, 0.0, sq), axis=-1, keepdims=True)
                msq = jnp.where(lo, s_lo, s_hi) * (1.0 / HEAD_DIM)
                r = lax.rsqrt(msq + RMS_EPS) * gain
            else:
                r = jnp.where(lo, lax.rsqrt(s_lo * (1.0 / HEAD_DIM) + RMS_EPS) * gain, 1.0)
            outs.append(sl * r)
        return jnp.concatenate(outs, axis=1)

    qn_ref[...] = gnorm(proj(C_QN, W_QN), C_QN, ["both"] * N_NSA_HEADS).astype(BF16)
    kvn_ref[...] = gnorm(proj(C_KVN, W_KVN), C_KVN, [None, "lo", "lo"]).astype(BF16)
    z_ref[...] = proj(C_Z, W_Z).astype(BF16)
    xbc_ref[...] = proj(C_XBC, W_XBC)
    qf_ref[...] = gnorm(proj(C_QF, W_QF), C_QF, ["both"] * N_FOX_HEADS).astype(BF16)
    kf_ref[...] = gnorm(proj(C_KF, W_KF), C_KF, ["both"] * (W_KF // LANES)).astype(BF16)
    vf_ref[...] = proj(C_VF, W_VF).astype(BF16)

    v = proj(C_SM, W_SM) + sbias_ref[...]
    e = jnp.exp(-jnp.abs(v))
    l1p = jnp.log1p(e)
    softplus = jnp.maximum(v, 0.0) + l1p
    log_sig = jnp.minimum(v, 0.0) - l1p
    sig = jnp.where(v >= 0.0, 1.0 / (1.0 + e), e / (1.0 + e))
    sm_ref[...] = jnp.where(lane < SM_F, softplus,
                            jnp.where(lane < SM_G, log_sig,
                                      jnp.where(lane < SM_G + 3 * N_NSA_HEADS, sig, 0.0)))


def _inproj(x2, g, w, gains, sbias):
    n, d = x2.shape
    tm = min(TM_PROJ, n)
    row = lambda wd: pl.BlockSpec((tm, wd), lambda i: (i, 0))
    full = lambda a: pl.BlockSpec(a.shape, lambda i: (0, 0))
    outs = [(W_QN, BF16), (W_KVN, BF16), (W_Z, BF16), (W_XBC, F32),
            (W_QF, BF16), (W_KF, BF16), (W_VF, BF16), (W_SM, F32)]
    return pl.pallas_call(
        _inproj_body,
        grid=(n // tm,),
        in_specs=[row(d), full(g), full(w), full(gains), full(sbias)],
        out_specs=[row(wd) for wd, _ in outs],
        out_shape=[jax.ShapeDtypeStruct((n, wd), dt) for wd, dt in outs],
        compiler_params=_cparams(("parallel",)),
        name="inproj",
    )(x2, g, w, gains, sbias)


def _pack_in_weights(w_in, nsa_q_norm, nsa_k_norm, fox_q_norm, fox_k_norm, dt_bias, f_bias):
    d = w_in.shape[0]
    pts = []
    acc = 0
    for s in IN_SIZES[:-1]:
        acc += s
        pts.append(acc)
    q_n, kv_n, g_n, z, xbc, dt, q_f, k_f, v_f, f_f = jnp.split(w_in, pts, axis=1)
    zw = jnp.zeros((d, HEAD_DIM), w_in.dtype)
    zg = jnp.zeros((HEAD_DIM,), F32)
    one = jnp.ones((HEAD_DIM,), F32)
    scale = HEAD_DIM ** -0.5
    cols, gains = [], []
    for h in range(N_NSA_HEADS):
        cols += [q_n[:, HEAD_DIM * h:HEAD_DIM * (h + 1)], zw]
        gains += [nsa_q_norm * scale, zg]
    cols.append(kv_n)
    gains += [one, one, nsa_k_norm, one, nsa_k_norm, one]
    cols += [z, xbc]
    gains += [jnp.ones((W_Z + W_XBC,), F32)]
    for h in range(N_FOX_HEADS):
        qh = q_f[:, HEAD_DIM * h:HEAD_DIM * (h + 1)]
        cols += [qh, zw] if h % 2 == 0 else [zw, qh]
        gains += [fox_q_norm * scale, zg] if h % 2 == 0 else [zg, fox_q_norm * scale]
    cols += [k_f, v_f]
    gains += [jnp.tile(fox_k_norm, N_FOX_HEADS), jnp.ones((W_VF,), F32)]
    pad = W_SM - (N_SSD_HEADS + N_FOX_HEADS + 3 * N_NSA_HEADS)
    cols += [dt, f_f, g_n, jnp.zeros((d, pad), w_in.dtype)]
    gains += [jnp.ones((W_SM,), F32)]
    w = jnp.concatenate(cols, axis=1).astype(BF16)
    gain = jnp.concatenate([g.astype(F32) for g in gains]).reshape(1, D_INP)
    sbias = jnp.concatenate([dt_bias.astype(F32), f_bias.astype(F32),
                             jnp.zeros((W_SM - N_SSD_HEADS - N_FOX_HEADS,), F32)]).reshape(1, W_SM)
    return w, gain, sbias


def _ssd_body(xbc_ref, z_ref, sm_ref, cw_ref, alog_ref, dexp_ref, ng_ref, e_ref, bm_ref,
              o_ref, ccol_ref, crow_ref, xw_ref, st_ref, fc_ref):
    L = xbc_ref.shape[1]
    c = pl.program_id(1)

    @pl.when(c == 0)
    def _():
        xw_ref[0:SUBLANES, :] = jnp.zeros((SUBLANES, W_XBC), F32)
        st_ref[...] = jnp.zeros_like(st_ref)
        fc_ref[...] = jnp.zeros_like(fc_ref)

    xw_ref[SUBLANES:SUBLANES + L, :] = xbc_ref[0]
    acc = jnp.broadcast_to(cw_ref[SSD_CONV:SSD_CONV + 1, :], (L, W_XBC))
    for k in range(SSD_CONV):
        r0 = SUBLANES - (SSD_CONV - 1) + k
        acc = acc + cw_ref[k:k + 1, :] * xw_ref[r0:r0 + L, :]
    xw_ref[0:SUBLANES, :] = xw_ref[L:L + SUBLANES, :]
    u = acc * (1.0 / (1.0 + jnp.exp(-acc)))
    xs = u[:, 0:D_SSD]
    bmat = u[:, D_SSD:D_SSD + LANES]
    cmat = u[:, D_SSD + LANES:D_SSD + 2 * LANES]

    lane = lax.broadcasted_iota(I32, (L, LANES), 1)
    sm = sm_ref[0]
    a_row = jnp.where(lane[0:1] < N_SSD_HEADS, -jnp.exp(alog_ref[...]), 0.0)
    is_dt = lane < SM_F
    is_f = (lane >= SM_F) & (lane < SM_G)
    vals = jnp.where(is_dt, sm * a_row, jnp.where(is_f, sm, 0.0))
    ri = lax.broadcasted_iota(I32, (L, L), 0)
    ci = lax.broadcasted_iota(I32, (L, L), 1)
    causal = ci <= ri
    tril = jnp.where(causal, 1.0, 0.0).astype(F32)
    cs_all = _dot(tril, vals, precision=HIGHEST)

    ccol = jnp.where(is_f, cs_all + fc_ref[...], 0.0)
    fc_ref[...] = ccol[L - 1:L, :]
    ccol_ref[0] = ccol
    crow_ref[0] = ccol.T[SM_F:SM_F + SUBLANES, :]

    cs = jnp.where(is_dt, cs_all, 0.0)
    cs_t = cs.T
    cs_exp = _dot(cs, e_ref[...], precision=HIGHEST)
    dt_exp = _dot(jnp.where(is_dt, sm, 0.0), e_ref[...], precision=HIGHEST)
    xdt = xs * dt_exp
    decay_out = jnp.exp(cs_exp)
    cs_last = cs_exp[L - 1:L, :]
    dte = jnp.exp(cs_last - cs_exp)
    chunk_decay = jnp.exp(cs_last)

    lo = lane < HEAD_DIM
    b_bf = bmat.astype(BF16)
    xdt_bf = xdt.astype(BF16)
    c_bf = cmat.astype(BF16)
    sc = [_dot_nt(jnp.where(lo, cmat, 0.0).astype(BF16), b_bf),
          _dot_nt(jnp.where(lo, 0.0, cmat).astype(BF16), b_bf)]
    y_pairs = []
    for pr in range(N_SSD_HEADS // 2):
        ys = []
        for hh in range(2):
            h = 2 * pr + hh
            grp = h // (N_SSD_HEADS // SSD_GROUPS)
            seg = cs[:, h:h + 1] - cs_t[h:h + 1, :]
            lmat = jnp.exp(jnp.where(causal, seg, NEG_INF))
            m = (sc[grp] * lmat).astype(BF16)
            ys.append(_dot(m, xdt_bf[:, LANES * pr:LANES * (pr + 1)]))
        y_pairs.append(jnp.where(lo, ys[0], ys[1]))
    y_diag = jnp.concatenate(y_pairs, axis=1)

    state = st_ref[...]
    y_off = _dot(c_bf, state.astype(BF16)) * decay_out
    bt = bmat.T.astype(BF16)
    st_new = _dot(bt, (xdt * dte).astype(BF16))
    st_ref[...] = state * chunk_decay + bm_ref[...] * st_new

    y = y_diag + y_off + xs * dexp_ref[...]
    zf = z_ref[0].astype(F32)
    y = y * (zf * (1.0 / (1.0 + jnp.exp(-zf))))
    ms = jnp.mean(y * y, axis=-1, keepdims=True)
    o_ref[0] = (y * lax.rsqrt(ms + RMS_EPS) * ng_ref[...]).astype(BF16)


def _ssd(xbc, z, sm, conv_w, conv_b, a_log, d_skip, norm_g):
    b, s, _ = xbc.shape
    L = min(SSD_L, s)
    cw = jnp.concatenate([conv_w.astype(F32), conv_b.astype(F32)[None, :],
                          jnp.zeros((SUBLANES - SSD_CONV - 1, W_XBC), F32)], axis=0)
    alog = jnp.concatenate([a_log.astype(F32), jnp.zeros((LANES - N_SSD_HEADS,), F32)]).reshape(1, LANES)
    dexp = jnp.repeat(d_skip.astype(F32), HEAD_DIM).reshape(1, D_SSD)
    ng = norm_g.astype(F32).reshape(1, D_SSD)
    hid = jnp.arange(LANES)[:, None]
    col = jnp.arange(D_SSD)[None, :]
    emat = ((col // HEAD_DIM) == hid).astype(F32)
    heads_per_group = N_SSD_HEADS // SSD_GROUPS
    bmask = ((hid // SSD_D_STATE) == (col // (HEAD_DIM * heads_per_group))).astype(F32)
    blk = lambda wd: pl.BlockSpec((1, L, wd), lambda bi, ci: (bi, ci, 0))
    full = lambda a: pl.BlockSpec(a.shape, lambda bi, ci: (0, 0))
    return pl.pallas_call(
        _ssd_body,
        grid=(b, s // L),
        in_specs=[blk(W_XBC), blk(D_SSD), blk(LANES), full(cw), full(alog), full(dexp), full(ng),
                  full(emat), full(bmask)],
        out_specs=[blk(D_SSD), blk(LANES), pl.BlockSpec((1, SUBLANES, L), lambda bi, ci: (bi, 0, ci))],
        out_shape=[jax.ShapeDtypeStruct((b, s, D_SSD), BF16),
                   jax.ShapeDtypeStruct((b, s, LANES), F32),
                   jax.ShapeDtypeStruct((b, SUBLANES, s), F32)],
        scratch_shapes=[pltpu.VMEM((L + SUBLANES, W_XBC), F32),
                        pltpu.VMEM((LANES, D_SSD), F32),
                        pltpu.VMEM((1, LANES), F32)],
        compiler_params=_cparams(("parallel", "arbitrary")),
        name="ssd_scan",
    )(xbc, z, sm, cw, alog, dexp, ng, emat, bmask)


def _fox_body(q_ref, k_ref, v_ref, crow_ref, o_ref, m_ref, l_ref, acc_ref):
    tq = q_ref.shape[1]
    tk = tq
    qi = pl.program_id(1)
    m_ref[...] = jnp.full_like(m_ref, NEG_INF)
    l_ref[...] = jnp.zeros_like(l_ref)
    acc_ref[...] = jnp.zeros_like(acc_ref)
    ri = lax.broadcasted_iota(I32, (tq, tk), 0)
    ci = lax.broadcasted_iota(I32, (tq, tk), 1)
    diag_mask = ci <= ri

    def tile(j, masked):
        k0 = pl.multiple_of(j * tk, tk)
        for pr in range(N_FOX_HEADS // 2):
            kslab = k_ref[0, pl.ds(k0, tk), LANES * pr:LANES * (pr + 1)]
            vslab = v_ref[0, pl.ds(k0, tk), LANES * pr:LANES * (pr + 1)]
            for hh in range(2):
                h = 2 * pr + hh
                q = q_ref[0, :, LANES * h:LANES * (h + 1)]
                s = _dot_nt(q, kslab)
                s = s - crow_ref[0, h:h + 1, pl.ds(k0, tk)]
                if masked:
                    s = jnp.where(diag_mask, s, NEG_INF)
                m_prev = m_ref[h]
                m_new = jnp.maximum(m_prev, jnp.max(s, axis=-1, keepdims=True))
                alpha = jnp.exp(m_prev - m_new)
                p = jnp.exp(s - m_new)
                l_ref[h] = alpha * l_ref[h] + jnp.sum(p, axis=-1, keepdims=True)
                acc_ref[h] = alpha * acc_ref[h] + _dot(p.astype(BF16), vslab)
                m_ref[h] = m_new

    def body(j, carry):
        tile(j, False)
        return carry

    lax.fori_loop(0, qi, body, 0)
    tile(qi, True)

    lane = lax.broadcasted_iota(I32, (tq, LANES), 1)
    lo = lane < HEAD_DIM
    outs = []
    for pr in range(N_FOX_HEADS // 2):
        tiny = jnp.finfo(F32).tiny
        o0 = acc_ref[2 * pr] / jnp.maximum(l_ref[2 * pr], tiny)
        o1 = acc_ref[2 * pr + 1] / jnp.maximum(l_ref[2 * pr + 1], tiny)
        outs.append(jnp.where(lo, o0, o1))
    o_ref[0] = jnp.concatenate(outs, axis=1).astype(BF16)


def _fox(qf, kf, vf, crow):
    b, s, _ = qf.shape
    tq = min(FOX_TQ, s)
    return pl.pallas_call(
        _fox_body,
        grid=(b, s // tq),
        in_specs=[pl.BlockSpec((1, tq, W_QF), lambda bi, qi: (bi, qi, 0)),
                  pl.BlockSpec((1, s, W_KF), lambda bi, qi: (bi, 0, 0)),
                  pl.BlockSpec((1, s, W_VF), lambda bi, qi: (bi, 0, 0)),
                  pl.BlockSpec((1, SUBLANES, s), lambda bi, qi: (bi, 0, 0))],
        out_specs=pl.BlockSpec((1, tq, D_FOX), lambda bi, qi: (bi, qi, 0)),
        out_shape=jax.ShapeDtypeStruct((b, s, D_FOX), BF16),
        scratch_shapes=[pltpu.VMEM((N_FOX_HEADS, tq, 1), F32),
                        pltpu.VMEM((N_FOX_HEADS, tq, 1), F32),
                        pltpu.VMEM((N_FOX_HEADS, tq, LANES), F32)],
        compiler_params=_cparams(("parallel", "arbitrary")),
        name="fox_attn",
    )(qf, kf, vf, crow)


def _cmp_body(gk_ref, gv_ref, wkt_ref, wkb_ref, wvt_ref, wvb_ref, pe_ref, gain_ref, o_ref):
    ncp = gk_ref.shape[1]
    gk = gk_ref[0]
    gv = gv_ref[0]
    top = _dot(gk, wkt_ref[...]) + _dot(gv, wvt_ref[...])
    bot = _dot(gk, wkb_ref[...]) + _dot(gv, wvb_ref[...])
    bot_next = pltpu.roll(bot, ncp - 1, 0)
    raw = top + bot_next + pe_ref[0:1, :]
    lane = lax.broadcasted_iota(I32, raw.shape, 1)
    lo = lane < HEAD_DIM
    msq = jnp.sum(jnp.where(lo, raw * raw, 0.0), axis=-1, keepdims=True) * (1.0 / HEAD_DIM)
    out = jnp.where(lo, raw * lax.rsqrt(msq + RMS_EPS) * gain_ref[...], raw)
    row = lax.broadcasted_iota(I32, raw.shape, 0)
    o_ref[0] = jnp.where(row < ncp - 1, out, 0.0).astype(BF16)


def _nsa_compress(kvn, pe_k, pe_v, w_ck, w_cv, k_norm):
    b, s, _ = kvn.shape
    half = NSA_CMP_STRIDE * HEAD_DIM
    ng = s // NSA_CMP_STRIDE
    gk = kvn[:, :, 0:HEAD_DIM].reshape(b, ng, half)
    gv = kvn[:, :, HEAD_DIM:2 * HEAD_DIM].reshape(b, ng, half)
    zc = jnp.zeros((half, HEAD_DIM), F32)
    wk = w_ck.astype(F32)
    wv = w_cv.astype(F32)
    wkt = jnp.concatenate([wk[:half], zc], axis=1).astype(BF16)
    wkb = jnp.concatenate([wk[half:], zc], axis=1).astype(BF16)
    wvt = jnp.concatenate([zc, wv[:half]], axis=1).astype(BF16)
    wvb = jnp.concatenate([zc, wv[half:]], axis=1).astype(BF16)
    pe_const = jnp.concatenate([jnp.dot(pe_k.reshape(1, -1).astype(F32), wk, precision=HIGHEST),
                                jnp.dot(pe_v.reshape(1, -1).astype(F32), wv, precision=HIGHEST)], axis=1)
    pe_const = jnp.concatenate([pe_const, jnp.zeros((SUBLANES - 1, LANES), F32)], axis=0)
    gain = jnp.concatenate([k_norm.astype(F32), jnp.ones((HEAD_DIM,), F32)]).reshape(1, LANES)
    blk = pl.BlockSpec((1, ng, half), lambda bi: (bi, 0, 0))
    full = lambda a: pl.BlockSpec(a.shape, lambda bi: (0, 0))
    return pl.pallas_call(
        _cmp_body,
        grid=(b,),
        in_specs=[blk, blk, full(wkt), full(wkb), full(wvt), full(wvb), full(pe_const), full(gain)],
        out_specs=pl.BlockSpec((1, ng, LANES), lambda bi: (bi, 0, 0)),
        out_shape=jax.ShapeDtypeStruct((b, ng, LANES), BF16),
        compiler_params=_cparams(("parallel",)),
        name="nsa_compress",
    )(gk, gv, wkt, wkb, wvt, wvb, pe_const, gain)


def _nsa_body(q_ref, kv_ref, kc_ref, sm_ref, ovt_ref, e_ref, o_ref,
              qs_ref, sc_ref, m_ref, l_ref, acc_ref, *, n_slc, top_n):
    tq = q_ref.shape[1]
    tk = e_ref.shape[2]
    ncp = kc_ref.shape[1]
    nh = N_NSA_HEADS
    rows = nh * tq
    qi = pl.program_id(1)
    t0 = qi * tq
    tiny = jnp.finfo(F32).tiny

    for h in range(nh):
        qs_ref[h * tq:(h + 1) * tq, :] = q_ref[0, :, LANES * h:LANES * (h + 1)]
    qs = qs_ref[...]

    kc = kc_ref[0]
    s_c = _dot_nt(qs, kc).reshape(nh, tq, ncp)
    t_q = t0 + lax.broadcasted_iota(I32, (1, tq, ncp), 1)
    c_i = lax.broadcasted_iota(I32, (1, tq, ncp), 2)
    valid_c = (c_i * NSA_CMP_STRIDE + (NSA_CMP_BLOCK - 1) <= t_q) & (c_i < ncp - 1)
    s_c = jnp.where(valid_c, s_c, NEG_INF)
    m_c = jnp.max(s_c, axis=-1, keepdims=True)
    e_c = jnp.exp(s_c - jnp.where(m_c == NEG_INF, 0.0, m_c))
    p_c = e_c / jnp.maximum(jnp.sum(e_c, axis=-1, keepdims=True), tiny)
    o_c = _dot(p_c.reshape(rows, ncp).astype(BF16), kc)

    p_sum = jnp.sum(p_c, axis=0)
    imp_t = _dot_nt(ovt_ref[...], p_sum, precision=HIGHEST)
    n_i = lax.broadcasted_iota(I32, (LANES, tq), 0)
    t_l = t0 + lax.broadcasted_iota(I32, (LANES, tq), 1)
    avail = n_i * NSA_SLC_BLOCK <= t_l
    forced = (n_i * NSA_SLC_BLOCK <= t_l) & (t_l < (n_i + 1) * NSA_SLC_BLOCK) | (n_i == 0)
    score = jnp.where(avail, jnp.where(forced, jnp.inf, imp_t), NEG_INF)
    sc_ref[...] = score
    ns_pad = ((n_slc + SUBLANES - 1) // SUBLANES) * SUBLANES
    sc = sc_ref[0:ns_pad, :]
    n_s = n_i[0:ns_pad, :]
    cnt = jnp.zeros((ns_pad, tq), F32)
    for mth in range(n_slc):
        r = sc_ref[mth:mth + 1, :]
        cnt = cnt + jnp.where(r > sc, 1.0, jnp.where((r == sc) & (n_s > mth), 1.0, 0.0))
    sel_t = jnp.where(avail[0:ns_pad, :] & (cnt < float(top_n)), 1.0, 0.0)
    sc_ref[0:ns_pad, :] = sel_t
    if ns_pad < LANES:
        sc_ref[ns_pad:LANES, :] = jnp.zeros((LANES - ns_pad, tq), F32)
    sel = sc_ref[...].T.astype(BF16)

    t_row = t0 + lax.broadcasted_iota(I32, (1, tq, tk), 1)
    k_col = lax.broadcasted_iota(I32, (1, tq, tk), 2)

    def attend(br, j, slab, mask):
        k0 = pl.multiple_of(j * tk, tk)
        kv = kv_ref[0, pl.ds(k0, tk), LANES * slab:LANES * (slab + 1)]
        s = _dot_nt(qs, kv).reshape(nh, tq, tk) + jnp.where(mask, 0.0, NEG_INF)
        m_prev = m_ref[br]
        m_new = jnp.maximum(m_prev, jnp.max(s, axis=-1, keepdims=True))
        m_safe = jnp.where(m_new == NEG_INF, 0.0, m_new)
        alpha = jnp.exp(m_prev - m_safe)
        p = jnp.exp(s - m_safe)
        l_ref[br] = alpha * l_ref[br] + jnp.sum(p, axis=-1, keepdims=True)
        pv = _dot(p.reshape(rows, tk).astype(BF16), kv).reshape(nh, tq, LANES)
        acc_ref[br] = alpha * acc_ref[br] + pv
        m_ref[br] = m_new

    m_ref[...] = jnp.full_like(m_ref, NEG_INF)
    l_ref[...] = jnp.zeros_like(l_ref)
    acc_ref[...] = jnp.zeros_like(acc_ref)

    def sel_body(j, carry):
        pos = j * tk + k_col
        blk_mask = _dot(sel, e_ref[j]) > 0.5
        attend(0, j, 1, blk_mask[None] & (pos <= t_row))
        return carry

    lax.fori_loop(0, (t0 + tq + tk - 1) // tk, sel_body, 0)

    def win_body(j, carry):
        pos = j * tk + k_col
        attend(1, j, 2, (pos <= t_row) & (pos > t_row - NSA_WINDOW))
        return carry

    w_lo = jnp.maximum(t0 - (NSA_WINDOW - 1), 0) // tk
    lax.fori_loop(w_lo, (t0 + tq + tk - 1) // tk, win_body, 0)

    sm = sm_ref[0]
    o_c3 = o_c.reshape(nh, tq, LANES)
    lane = lax.broadcasted_iota(I32, (tq, LANES), 1)
    lo = lane < HEAD_DIM
    heads = []
    for h in range(nh):
        g0 = sm[:, SM_G + 3 * h:SM_G + 3 * h + 1]
        g1 = sm[:, SM_G + 3 * h + 1:SM_G + 3 * h + 2]
        g2 = sm[:, SM_G + 3 * h + 2:SM_G + 3 * h + 3]
        o_s = acc_ref[0, h] / jnp.maximum(l_ref[0, h], tiny)
        o_w = acc_ref[1, h] / jnp.maximum(l_ref[1, h], tiny)
        heads.append(g0 * o_c3[h] + g1 * o_s + g2 * o_w)
    outs = []
    for pr in range(nh // 2):
        outs.append(jnp.where(lo, pltpu.roll(heads[2 * pr], HEAD_DIM, 1), heads[2 * pr + 1]))
    o_ref[0] = jnp.concatenate(outs, axis=1).astype(BF16)


def _nsa(qn, kvn, kc, sm):
    b, s, _ = qn.shape
    tq = min(NSA_TQ, s)
    tk = min(NSA_TK, s)
    n_cmp = (s - NSA_CMP_BLOCK) // NSA_CMP_STRIDE + 1
    n_slc = s // NSA_SLC_BLOCK
    ncp = kc.shape[1]
    assert n_cmp == ncp - 1 and n_slc <= LANES
    top_n = min(NSA_TOP_N, n_slc)
    c_start = jnp.arange(ncp) * NSA_CMP_STRIDE
    n_start = jnp.arange(LANES) * NSA_SLC_BLOCK
    ovt = ((c_start[None, :] < n_start[:, None] + NSA_SLC_BLOCK)
           & (c_start[None, :] + NSA_CMP_BLOCK > n_start[:, None])
           & (jnp.arange(ncp)[None, :] < n_cmp) & (jnp.arange(LANES)[:, None] < n_slc)).astype(F32)
    pos = jnp.arange(s).reshape(s // tk, 1, tk)
    emat = ((pos // NSA_SLC_BLOCK) == jnp.arange(LANES)[None, :, None]).astype(BF16)
    rows = N_NSA_HEADS * tq
    return pl.pallas_call(
        functools.partial(_nsa_body, n_slc=n_slc, top_n=top_n),
        grid=(b, s // tq),
        in_specs=[pl.BlockSpec((1, tq, W_QN), lambda bi, qi: (bi, qi, 0)),
                  pl.BlockSpec((1, s, W_KVN), lambda bi, qi: (bi, 0, 0)),
                  pl.BlockSpec((1, ncp, LANES), lambda bi, qi: (bi, 0, 0)),
                  pl.BlockSpec((1, tq, LANES), lambda bi, qi: (bi, qi, 0)),
                  pl.BlockSpec(ovt.shape, lambda bi, qi: (0, 0)),
                  pl.BlockSpec(emat.shape, lambda bi, qi: (0, 0, 0))],
        out_specs=pl.BlockSpec((1, tq, D_NSA), lambda bi, qi: (bi, qi, 0)),
        out_shape=jax.ShapeDtypeStruct((b, s, D_NSA), BF16),
        scratch_shapes=[pltpu.VMEM((rows, LANES), BF16),
                        pltpu.VMEM((LANES, tq), F32),
                        pltpu.VMEM((2, N_NSA_HEADS, tq, 1), F32),
                        pltpu.VMEM((2, N_NSA_HEADS, tq, 1), F32),
                        pltpu.VMEM((2, N_NSA_HEADS, tq, LANES), F32)],
        compiler_params=_cparams(("parallel", "arbitrary")),
        name="nsa_attn",
    )(qn, kvn, kc, sm, ovt, emat)


def _outproj_body(*refs, route):
    if route:
        (on_ref, os_ref, of_ref, x_ref, wn_ref, ws_ref, wf_ref, g_ref, wr_ref,
         xo_ref, h_ref, rt_ref) = refs
    else:
        on_ref, os_ref, of_ref, x_ref, wn_ref, ws_ref, wf_ref, g_ref, xo_ref, h_ref = refs
    x = (x_ref[...] + _dot(on_ref[...], wn_ref[...]) + _dot(os_ref[...], ws_ref[...])
         + _dot(of_ref[...], wf_ref[...]))
    xo_ref[...] = x
    ms = jnp.mean(x * x, axis=-1, keepdims=True)
    h = x * lax.rsqrt(ms + RMS_EPS) * g_ref[...]
    h_ref[...] = h.astype(h_ref.dtype)
    if route:
        logits = _dot(h, wr_ref[...], precision=HIGHEST)
        lane = lax.broadcasted_iota(I32, logits.shape, 1)
        lf = lane.astype(F32)
        lg = jnp.where(lane < N_EXPERTS, logits, NEG_INF)
        m1 = jnp.max(lg, axis=-1, keepdims=True)
        i1 = jnp.min(jnp.where(lg == m1, lf, float(LANES)), axis=-1, keepdims=True)
        lg2 = jnp.where(lf == i1, NEG_INF, lg)
        m2 = jnp.max(lg2, axis=-1, keepdims=True)
        i2 = jnp.min(jnp.where(lg2 == m2, lf, float(LANES)), axis=-1, keepdims=True)
        e21 = jnp.exp(m2 - m1)
        g1 = 1.0 / (1.0 + e21)
        g2 = e21 / (1.0 + e21)
        rt_ref[...] = jnp.where(lane == 0, i1, jnp.where(lane == 1, i2,
                                jnp.where(lane == 2, g1, jnp.where(lane == 3, g2, 0.0))))


def _outproj(o_nsa, o_ssd, o_fox, x2, w_out, g, w_router=None):
    n, d = x2.shape
    tm = min(TM_PROJ, n)
    route = w_router is not None
    wn = w_out[0:D_NSA].astype(BF16)
    ws = w_out[D_NSA:D_NSA + D_SSD].astype(BF16)
    wf = w_out[D_NSA + D_SSD:].astype(BF16)
    gg = g.astype(F32).reshape(1, d)
    row = lambda wd: pl.BlockSpec((tm, wd), lambda i: (i, 0))
    full = lambda a: pl.BlockSpec(a.shape, lambda i: (0, 0))
    ins = [o_nsa, o_ssd, o_fox, x2, wn, ws, wf, gg]
    in_specs = [row(D_NSA), row(D_SSD), row(D_FOX), row(d), full(wn), full(ws), full(wf), full(gg)]
    out_specs = [row(d), row(d)]
    out_shape = [jax.ShapeDtypeStruct((n, d), F32),
                 jax.ShapeDtypeStruct((n, d), F32 if route else BF16)]
    if route:
        wr = jnp.concatenate([w_router.astype(F32), jnp.zeros((d, LANES - N_EXPERTS), F32)], axis=1)
        ins.append(wr)
        in_specs.append(full(wr))
        out_specs.append(row(LANES))
        out_shape.append(jax.ShapeDtypeStruct((n, LANES), F32))
    return pl.pallas_call(
        functools.partial(_outproj_body, route=route),
        grid=(n // tm,),
        in_specs=in_specs, out_specs=out_specs, out_shape=out_shape,
        compiler_params=_cparams(("parallel",)),
        name="outproj_route" if route else "outproj",
    )(*ins)


def _swiglu_acc(h, w1_ref, w3_ref, w2_ref, acc_ref, f):
    @pl.when(f == 0)
    def _():
        acc_ref[...] = jnp.zeros_like(acc_ref)

    a = _dot(h, w1_ref[...])
    b = _dot(h, w3_ref[...])
    mid = (a * (1.0 / (1.0 + jnp.exp(-a))) * b).astype(BF16)
    acc_ref[...] += _dot(mid, w2_ref[...])


def _ffn_body(h_ref, x_ref, w1_ref, w3_ref, w2_ref, o_ref, acc_ref):
    f = pl.program_id(1)
    _swiglu_acc(h_ref[...], w1_ref, w3_ref, w2_ref, acc_ref, f)

    @pl.when(f == pl.num_programs(1) - 1)
    def _():
        o_ref[...] = x_ref[...] + acc_ref[...]


def _ffn_dense(h, x2, w1, w3, w2):
    n, d = x2.shape
    ff = w1.shape[1]
    tm = min(FFN_TM, n)
    tf = FFN_TF
    return pl.pallas_call(
        _ffn_body,
        grid=(n // tm, ff // tf),
        in_specs=[pl.BlockSpec((tm, d), lambda i, f: (i, 0)),
                  pl.BlockSpec((tm, d), lambda i, f: (i, 0)),
                  pl.BlockSpec((d, tf), lambda i, f: (0, f)),
                  pl.BlockSpec((d, tf), lambda i, f: (0, f)),
                  pl.BlockSpec((tf, d), lambda i, f: (f, 0))],
        out_specs=pl.BlockSpec((tm, d), lambda i, f: (i, 0)),
        out_shape=jax.ShapeDtypeStruct((n, d), F32),
        scratch_shapes=[pltpu.VMEM((tm, d), F32)],
        compiler_params=_cparams(("parallel", "arbitrary")),
        name="ffn_dense",
    )(h, x2, w1.astype(BF16), w3.astype(BF16), w2.astype(BF16))


def _gffn_body(be_ref, xs_ref, w1_ref, w3_ref, w2_ref, o_ref, acc_ref):
    f = pl.program_id(1)
    _swiglu_acc(xs_ref[...].astype(BF16), w1_ref.at[0], w3_ref.at[0], w2_ref.at[0], acc_ref, f)

    @pl.when(f == pl.num_programs(1) - 1)
    def _():
        o_ref[...] = acc_ref[...]


def _ffn_grouped(xs, blk_e, w1, w3, w2):
    p, d = xs.shape
    ff = w1.shape[2]
    tm = MOE_TM
    tf = FFN_TF
    grid_spec = pltpu.PrefetchScalarGridSpec(
        num_scalar_prefetch=1,
        grid=(p // tm, ff // tf),
        in_specs=[pl.BlockSpec((tm, d), lambda i, f, be: (i, 0)),
                  pl.BlockSpec((1, d, tf), lambda i, f, be: (be[i], 0, f)),
                  pl.BlockSpec((1, d, tf), lambda i, f, be: (be[i], 0, f)),
                  pl.BlockSpec((1, tf, d), lambda i, f, be: (be[i], f, 0))],
        out_specs=pl.BlockSpec((tm, d), lambda i, f, be: (i, 0)),
        scratch_shapes=[pltpu.VMEM((tm, d), F32)])
    return pl.pallas_call(
        _gffn_body,
        grid_spec=grid_spec,
        out_shape=jax.ShapeDtypeStruct((p, d), F32),
        compiler_params=_cparams(("parallel", "arbitrary")),
        name="ffn_grouped",
    )(blk_e, xs, w1.astype(BF16), w3.astype(BF16), w2.astype(BF16))


def _row_copy(src, dst, sem):
    return pltpu.make_async_copy(src, dst, sem)


def _dispatch_body(tok_ref, h_hbm, o_ref, buf_ref, sem):
    tt = o_ref.shape[0]

    def issue(r, carry):
        t = tok_ref[0, 0, r]
        _row_copy(h_hbm.at[pl.ds(t, 1)], buf_ref.at[pl.ds(r, 1)], sem).start()
        return carry

    lax.fori_loop(0, tt, issue, 0, unroll=8)

    def drain(r, carry):
        _row_copy(h_hbm.at[pl.ds(0, 1)], buf_ref.at[pl.ds(0, 1)], sem).wait()
        return carry

    lax.fori_loop(0, tt, drain, 0)
    o_ref[...] = buf_ref[...].astype(o_ref.dtype)


def _dispatch(h, tok3):
    n, d = h.shape
    nt, _, tt = tok3.shape
    return pl.pallas_call(
        _dispatch_body,
        grid=(nt,),
        in_specs=[pl.BlockSpec((1, 1, tt), lambda i: (i, 0, 0), memory_space=pltpu.SMEM),
                  pl.BlockSpec(memory_space=pl.ANY)],
        out_specs=pl.BlockSpec((tt, d), lambda i: (i, 0)),
        out_shape=jax.ShapeDtypeStruct((nt * tt, d), BF16),
        scratch_shapes=[pltpu.VMEM((tt, d), h.dtype), pltpu.SemaphoreType.DMA(())],
        compiler_params=_cparams(("arbitrary",)),
        name="moe_dispatch",
    )(tok3, h)


def _combine_body(dest_ref, x_ref, rt_ref, y_hbm, o_ref, buf_ref, sem):
    tt = x_ref.shape[0]

    def issue(r, carry):
        for k in range(TOP_K):
            d = dest_ref[0, k, r]
            _row_copy(y_hbm.at[pl.ds(d, 1)], buf_ref.at[k, pl.ds(r, 1)], sem).start()
        return carry

    lax.fori_loop(0, tt, issue, 0, unroll=8)

    def drain(r, carry):
        for k in range(TOP_K):
            _row_copy(y_hbm.at[pl.ds(0, 1)], buf_ref.at[k, pl.ds(0, 1)], sem).wait()
        return carry

    lax.fori_loop(0, tt, drain, 0)
    rt = rt_ref[...]
    o_ref[...] = x_ref[...] + rt[:, 2:3] * buf_ref[0] + rt[:, 3:4] * buf_ref[1]


def _combine(x2, route, dest3, y):
    n, d = x2.shape
    tt = dest3.shape[2]
    return pl.pallas_call(
        _combine_body,
        grid=(n // tt,),
        in_specs=[pl.BlockSpec((1, TOP_K, tt), lambda i: (i, 0, 0), memory_space=pltpu.SMEM),
                  pl.BlockSpec((tt, d), lambda i: (i, 0)),
                  pl.BlockSpec((tt, LANES), lambda i: (i, 0)),
                  pl.BlockSpec(memory_space=pl.ANY)],
        out_specs=pl.BlockSpec((tt, d), lambda i: (i, 0)),
        out_shape=jax.ShapeDtypeStruct((n, d), F32),
        scratch_shapes=[pltpu.VMEM((TOP_K, tt, d), F32), pltpu.SemaphoreType.DMA(())],
        compiler_params=_cparams(("arbitrary",)),
        name="moe_combine",
    )(dest3, x2, route, y)


def _moe(h, x2, route, w1, w3, w2):
    n, d = x2.shape
    tm = MOE_TM
    tt = min(MOE_TT, n)
    n_rows = n * TOP_K
    flat_e = route[:, 0:TOP_K].astype(I32).reshape(-1)
    onehot = (flat_e[:, None] == jnp.arange(N_EXPERTS)[None, :]).astype(I32)
    csum = jnp.cumsum(onehot, axis=0)
    rank = jnp.sum(csum * onehot, axis=1) - 1
    counts = csum[-1]
    padded = ((counts + tm - 1) // tm) * tm
    pend = jnp.cumsum(padded)
    pstart = pend - padded
    dest = jnp.sum(onehot * pstart[None, :], axis=1) + rank
    n_blk = -(-n_rows // tm) + N_EXPERTS
    blk_e = jnp.minimum(jnp.searchsorted(pend, jnp.arange(n_blk) * tm, side="right"),
                        N_EXPERTS - 1).astype(I32)
    dest3 = dest.reshape(n // tt, tt, TOP_K).transpose(0, 2, 1).astype(I32)
    pad_i = jnp.arange(tm)[None, :]
    pad_key = jnp.where(pad_i < (padded - counts)[:, None], jnp.arange(N_EXPERTS)[:, None], N_EXPERTS)
    keys = jnp.concatenate([flat_e, pad_key.reshape(-1).astype(I32)])
    toks = jnp.concatenate([jnp.arange(n_rows, dtype=I32) // TOP_K, jnp.zeros((N_EXPERTS * tm,), I32)])
    _, tok_sorted = lax.sort((keys, toks), num_keys=1, is_stable=True)
    tok3 = tok_sorted[:n_blk * tm].reshape(-1, 1, tt)
    xs = _dispatch(h, tok3)
    y = _ffn_grouped(xs, blk_e, w1, w3, w2)
    return _combine(x2, route, dest3, y)


def kernel(x, attn_norm, w_in, nsa_q_norm, nsa_k_norm, nsa_cmp_pe_k, nsa_cmp_pe_v, nsa_cmp_w_k,
           nsa_cmp_w_v, ssd_conv_w, ssd_conv_b, ssd_dt_bias, ssd_a_log, ssd_d, ssd_norm, fox_q_norm,
           fox_k_norm, fox_f_bias, w_out, ffn_norm, ffn_w1, ffn_w3, ffn_w2, moe_router, moe_w1,
           moe_w3, moe_w2):
    b, s, d = x.shape
    n = b * s
    depth = w_in.shape[0]
    x2 = x.reshape(n, d).astype(F32)
    for layer in range(depth):
        w, gains, sbias = _pack_in_weights(w_in[layer], nsa_q_norm[layer], nsa_k_norm[layer],
                                           fox_q_norm[layer], fox_k_norm[layer],
                                           ssd_dt_bias[layer], fox_f_bias[layer])
        g = attn_norm[layer].astype(F32).reshape(1, d)
        qn, kvn, z, xbc, qf, kf, vf, sm = _inproj(x2, g, w, gains, sbias)
        r3 = lambda a: a.reshape(b, s, a.shape[-1])
        kvn3, sm3 = r3(kvn), r3(sm)
        kc = _nsa_compress(kvn3, nsa_cmp_pe_k[layer], nsa_cmp_pe_v[layer],
                           nsa_cmp_w_k[layer], nsa_cmp_w_v[layer], nsa_k_norm[layer])
        o_nsa = _nsa(r3(qn), kvn3, kc, sm3)
        o_ssd, ccol, crow = _ssd(r3(xbc), r3(z), sm3, ssd_conv_w[layer], ssd_conv_b[layer],
                                 ssd_a_log[layer], ssd_d[layer], ssd_norm[layer])
        o_fox = _fox(r3(qf), r3(kf), r3(vf), crow)
        idx = layer // 2
        is_moe = layer % 2 == 1
        res = _outproj(o_nsa.reshape(n, D_NSA), o_ssd.reshape(n, D_SSD), o_fox.reshape(n, D_FOX),
                       x2, w_out[layer], ffn_norm[layer], moe_router[idx] if is_moe else None)
        if is_moe:
            x2, h, route = res
            x2 = _moe(h, x2, route, moe_w1[idx], moe_w3[idx], moe_w2[idx])
        else:
            x2, h = res
            x2 = _ffn_dense(h, x2, ffn_w1[idx], ffn_w3[idx], ffn_w2[idx])
    return x2.reshape(b, s, d).astype(x.dtype)
```

```python
import functools
import math

import jax
import jax.numpy as jnp
from jax import lax
from jax.experimental import pallas as pl
from jax.experimental.pallas import tpu as pltpu

F32 = jnp.float32
BF16 = jnp.bfloat16
I32 = jnp.int32
HIGHEST = lax.Precision.HIGHEST
NEG_INF = float("-inf")

LANES = 128
SUBLANES = 8
VMEM_BYTES_V7X = 64 * 1024 * 1024
VMEM_LIMIT = 48 * 1024 * 1024

HEAD_DIM = 64
N_NSA_HEADS = 4
N_SSD_HEADS = 8
N_FOX_HEADS = 4
D_NSA = N_NSA_HEADS * HEAD_DIM
D_SSD = N_SSD_HEADS * HEAD_DIM
D_FOX = N_FOX_HEADS * HEAD_DIM
NSA_N_KV_TENSORS = 6
NSA_CMP_BLOCK = 32
NSA_CMP_STRIDE = 16
NSA_SLC_BLOCK = 64
NSA_TOP_N = 16
NSA_WINDOW = 512
SSD_GROUPS = 2
SSD_D_STATE = 64
SSD_CONV = 4
SSD_CONV_DIM = D_SSD + 2 * SSD_GROUPS * SSD_D_STATE
N_EXPERTS = 8
TOP_K = 2
RMS_EPS = 1e-6
IN_SIZES = (D_NSA, NSA_N_KV_TENSORS * HEAD_DIM, 3 * N_NSA_HEADS,
            D_SSD, SSD_CONV_DIM, N_SSD_HEADS,
            D_FOX, D_FOX, D_FOX, N_FOX_HEADS)

C_QN, W_QN = 0, N_NSA_HEADS * LANES
C_KVN, W_KVN = C_QN + W_QN, NSA_N_KV_TENSORS * HEAD_DIM
C_Z, W_Z = C_KVN + W_KVN, D_SSD
C_XBC, W_XBC = C_Z + W_Z, SSD_CONV_DIM
C_QF, W_QF = C_XBC + W_XBC, N_FOX_HEADS * LANES
C_KF, W_KF = C_QF + W_QF, D_FOX
C_VF, W_VF = C_KF + W_KF, D_FOX
C_SM, W_SM = C_VF + W_VF, LANES
D_INP = C_SM + W_SM
SM_DT, SM_F, SM_G = 0, N_SSD_HEADS, N_SSD_HEADS + N_FOX_HEADS

TM_PROJ = 512
SSD_L = 128
FOX_TQ = 512
NSA_TQ = 128
NSA_TK = 512
FFN_TM = 512
FFN_TF = 1408
MOE_TM = 512
MOE_TT = 256


def _cparams(sem):
    return pltpu.CompilerParams(dimension_semantics=sem, vmem_limit_bytes=VMEM_LIMIT)


def _dot(a, b, **kw):
    return jnp.dot(a, b, preferred_element_type=F32, **kw)


def _dot_nt(a, b, **kw):
    return lax.dot_general(a, b, (((1,), (1,)), ((), ())), preferred_element_type=F32, **kw)


def _inproj_body(x_ref, g_ref, w_ref, gain_ref, sbias_ref,
                 qn_ref, kvn_ref, z_ref, xbc_ref, qf_ref, kf_ref, vf_ref, sm_ref):
    x = x_ref[...]
    ms = jnp.mean(x * x, axis=-1, keepdims=True)
    h = (x * lax.rsqrt(ms + RMS_EPS) * g_ref[...]).astype(BF16)
    lane = lax.broadcasted_iota(I32, (x.shape[0], LANES), 1)
    lo = lane < HEAD_DIM

    def proj(c0, wd):
        return _dot(h, w_ref[:, c0:c0 + wd])

    def gnorm(v, c0, modes):
        outs = []
        for s, mode in enumerate(modes):
            sl = v[:, LANES * s:LANES * (s + 1)]
            if mode is None:
                outs.append(sl)
                continue
            gain = gain_ref[:, c0 + LANES * s:c0 + LANES * (s + 1)]
            sq = sl * sl
            s_lo = jnp.sum(jnp.where(lo, sq, 0.0), axis=-1, keepdims=True)
            if mode == "both":
                s_hi = jnp.sum(jnp.where(lo, 0.0, sq), axis=-1, keepdims=True)
                msq = jnp.where(lo, s_lo, s_hi) * (1.0 / HEAD_DIM)
                r = lax.rsqrt(msq + RMS_EPS) * gain
            else:
                r = jnp.where(lo, lax.rsqrt(s_lo * (1.0 / HEAD_DIM) + RMS_EPS) * gain, 1.0)
            outs.append(sl * r)
        return jnp.concatenate(outs, axis=1)

    qn_ref[...] = gnorm(proj(C_QN, W_QN), C_QN, ["both"] * N_NSA_HEADS).astype(BF16)
    kvn_ref[...] = gnorm(proj(C_KVN, W_KVN), C_KVN, [None, "lo", "lo"]).astype(BF16)
    z_ref[...] = proj(C_Z, W_Z).astype(BF16)
    xbc_ref[...] = proj(C_XBC, W_XBC)
    qf_ref[...] = gnorm(proj(C_QF, W_QF), C_QF, ["both"] * N_FOX_HEADS).astype(BF16)
    kf_ref[...] = gnorm(proj(C_KF, W_KF), C_KF, ["both"] * (W_KF // LANES)).astype(BF16)
    vf_ref[...] = proj(C_VF, W_VF).astype(BF16)

    v = proj(C_SM, W_SM) + sbias_ref[...]
    e = jnp.exp(-jnp.abs(v))
    l1p = jnp.log1p(e)
    softplus = jnp.maximum(v, 0.0) + l1p
    log_sig = jnp.minimum(v, 0.0) - l1p
    sig = jnp.where(v >= 0.0, 1.0 / (1.0 + e), e / (1.0 + e))
    sm_ref[...] = jnp.where(lane < SM_F, softplus,
                            jnp.where(lane < SM_G, log_sig,
                                      jnp.where(lane < SM_G + 3 * N_NSA_HEADS, sig, 0.0)))


def _inproj(x2, g, w, gains, sbias):
    n, d = x2.shape
    tm = min(TM_PROJ, n)
    row = lambda wd: pl.BlockSpec((tm, wd), lambda i: (i, 0))
    full = lambda a: pl.BlockSpec(a.shape, lambda i: (0, 0))
    outs = [(W_QN, BF16), (W_KVN, BF16), (W_Z, BF16), (W_XBC, F32),
            (W_QF, BF16), (W_KF, BF16), (W_VF, BF16), (W_SM, F32)]
    return pl.pallas_call(
        _inproj_body,
        grid=(n // tm,),
        in_specs=[row(d), full(g), full(w), full(gains), full(sbias)],
        out_specs=[row(wd) for wd, _ in outs],
        out_shape=[jax.ShapeDtypeStruct((n, wd), dt) for wd, dt in outs],
        compiler_params=_cparams(("parallel",)),
        name="inproj",
    )(x2, g, w, gains, sbias)


def _pack_in_weights(w_in, nsa_q_norm, nsa_k_norm, fox_q_norm, fox_k_norm, dt_bias, f_bias):
    d = w_in.shape[0]
    pts = []
    acc = 0
    for s in IN_SIZES[:-1]:
        acc += s
        pts.append(acc)
    q_n, kv_n, g_n, z, xbc, dt, q_f, k_f, v_f, f_f = jnp.split(w_in, pts, axis=1)
    zw = jnp.zeros((d, HEAD_DIM), w_in.dtype)
    zg = jnp.zeros((HEAD_DIM,), F32)
    one = jnp.ones((HEAD_DIM,), F32)
    scale = HEAD_DIM ** -0.5
    cols, gains = [], []
    for h in range(N_NSA_HEADS):
        cols += [q_n[:, HEAD_DIM * h:HEAD_DIM * (h + 1)], zw]
        gains += [nsa_q_norm * scale, zg]
    cols.append(kv_n)
    gains += [one, one, nsa_k_norm, one, nsa_k_norm, one]
    cols += [z, xbc]
    gains += [jnp.ones((W_Z + W_XBC,), F32)]
    for h in range(N_FOX_HEADS):
        qh = q_f[:, HEAD_DIM * h:HEAD_DIM * (h + 1)]
        cols += [qh, zw] if h % 2 == 0 else [zw, qh]
        gains += [fox_q_norm * scale, zg] if h % 2 == 0 else [zg, fox_q_norm * scale]
    cols += [k_f, v_f]
    gains += [jnp.tile(fox_k_norm, N_FOX_HEADS), jnp.ones((W_VF,), F32)]
    pad = W_SM - (N_SSD_HEADS + N_FOX_HEADS + 3 * N_NSA_HEADS)
    cols += [dt, f_f, g_n, jnp.zeros((d, pad), w_in.dtype)]
    gains += [jnp.ones((W_SM,), F32)]
    w = jnp.concatenate(cols, axis=1).astype(BF16)
    gain = jnp.concatenate([g.astype(F32) for g in gains]).reshape(1, D_INP)
    sbias = jnp.concatenate([dt_bias.astype(F32), f_bias.astype(F32),
                             jnp.zeros((W_SM - N_SSD_HEADS - N_FOX_HEADS,), F32)]).reshape(1, W_SM)
    return w, gain, sbias


def _ssd_body(xbc_ref, z_ref, sm_ref, cw_ref, alog_ref, dexp_ref, ng_ref, e_ref, bm_ref,
              o_ref, ccol_ref, crow_ref, xw_ref, st_ref, fc_ref):
    L = xbc_ref.shape[1]
    c = pl.program_id(1)

    @pl.when(c == 0)
    def _():
        xw_ref[0:SUBLANES, :] = jnp.zeros((SUBLANES, W_XBC), F32)
        st_ref[...] = jnp.zeros_like(st_ref)
        fc_ref[...] = jnp.zeros_like(fc_ref)

    xw_ref[SUBLANES:SUBLANES + L, :] = xbc_ref[0]
    acc = jnp.broadcast_to(cw_ref[SSD_CONV:SSD_CONV + 1, :], (L, W_XBC))
    for k in range(SSD_CONV):
        r0 = SUBLANES - (SSD_CONV - 1) + k
        acc = acc + cw_ref[k:k + 1, :] * xw_ref[r0:r0 + L, :]
    xw_ref[0:SUBLANES, :] = xw_ref[L:L + SUBLANES, :]
    u = acc * (1.0 / (1.0 + jnp.exp(-acc)))
    xs = u[:, 0:D_SSD]
    bmat = u[:, D_SSD:D_SSD + LANES]
    cmat = u[:, D_SSD + LANES:D_SSD + 2 * LANES]

    lane = lax.broadcasted_iota(I32, (L, LANES), 1)
    sm = sm_ref[0]
    a_row = jnp.where(lane[0:1] < N_SSD_HEADS, -jnp.exp(alog_ref[...]), 0.0)
    is_dt = lane < SM_F
    is_f = (lane >= SM_F) & (lane < SM_G)
    vals = jnp.where(is_dt, sm * a_row, jnp.where(is_f, sm, 0.0))
    ri = lax.broadcasted_iota(I32, (L, L), 0)
    ci = lax.broadcasted_iota(I32, (L, L), 1)
    causal = ci <= ri
    tril = jnp.where(causal, 1.0, 0.0).astype(F32)
    cs_all = _dot(tril, vals, precision=HIGHEST)

    ccol = jnp.where(is_f, cs_all + fc_ref[...], 0.0)
    fc_ref[...] = ccol[L - 1:L, :]
    ccol_ref[0] = ccol
    crow_ref[0] = ccol.T[SM_F:SM_F + SUBLANES, :]

    cs = jnp.where(is_dt, cs_all, 0.0)
    cs_t = cs.T
    cs_exp = _dot(cs, e_ref[...], precision=HIGHEST)
    dt_exp = _dot(jnp.where(is_dt, sm, 0.0), e_ref[...], precision=HIGHEST)
    xdt = xs * dt_exp
    decay_out = jnp.exp(cs_exp)
    cs_last = cs_exp[L - 1:L, :]
    dte = jnp.exp(cs_last - cs_exp)
    chunk_decay = jnp.exp(cs_last)

    lo = lane < HEAD_DIM
    b_bf = bmat.astype(BF16)
    xdt_bf = xdt.astype(BF16)
    c_bf = cmat.astype(BF16)
    sc = [_dot_nt(jnp.where(lo, cmat, 0.0).astype(BF16), b_bf),
          _dot_nt(jnp.where(lo, 0.0, cmat).astype(BF16), b_bf)]
    y_pairs = []
    for pr in range(N_SSD_HEADS // 2):
        ys = []
        for hh in range(2):
            h = 2 * pr + hh
            grp = h // (N_SSD_HEADS // SSD_GROUPS)
            seg = cs[:, h:h + 1] - cs_t[h:h + 1, :]
            lmat = jnp.exp(jnp.where(causal, seg, NEG_INF))
            m = (sc[grp] * lmat).astype(BF16)
            ys.append(_dot(m, xdt_bf[:, LANES * pr:LANES * (pr + 1)]))
        y_pairs.append(jnp.where(lo, ys[0], ys[1]))
    y_diag = jnp.concatenate(y_pairs, axis=1)

    state = st_ref[...]
    y_off = _dot(c_bf, state.astype(BF16)) * decay_out
    bt = bmat.T.astype(BF16)
    st_new = _dot(bt, (xdt * dte).astype(BF16))
    st_ref[...] = state * chunk_decay + bm_ref[...] * st_new

    y = y_diag + y_off + xs * dexp_ref[...]
    zf = z_ref[0].astype(F32)
    y = y * (zf * (1.0 / (1.0 + jnp.exp(-zf))))
    ms = jnp.mean(y * y, axis=-1, keepdims=True)
    o_ref[0] = (y * lax.rsqrt(ms + RMS_EPS) * ng_ref[...]).astype(BF16)


def _ssd(xbc, z, sm, conv_w, conv_b, a_log, d_skip, norm_g):
    b, s, _ = xbc.shape
    L = min(SSD_L, s)
    cw = jnp.concatenate([conv_w.astype(F32), conv_b.astype(F32)[None, :],
                          jnp.zeros((SUBLANES - SSD_CONV - 1, W_XBC), F32)], axis=0)
    alog = jnp.concatenate([a_log.astype(F32), jnp.zeros((LANES - N_SSD_HEADS,), F32)]).reshape(1, LANES)
    dexp = jnp.repeat(d_skip.astype(F32), HEAD_DIM).reshape(1, D_SSD)
    ng = norm_g.astype(F32).reshape(1, D_SSD)
    hid = jnp.arange(LANES)[:, None]
    col = jnp.arange(D_SSD)[None, :]
    emat = ((col // HEAD_DIM) == hid).astype(F32)
    heads_per_group = N_SSD_HEADS // SSD_GROUPS
    bmask = ((hid // SSD_D_STATE) == (col // (HEAD_DIM * heads_per_group))).astype(F32)
    blk = lambda wd: pl.BlockSpec((1, L, wd), lambda bi, ci: (bi, ci, 0))
    full = lambda a: pl.BlockSpec(a.shape, lambda bi, ci: (0, 0))
    return pl.pallas_call(
        _ssd_body,
        grid=(b, s // L),
        in_specs=[blk(W_XBC), blk(D_SSD), blk(LANES), full(cw), full(alog), full(dexp), full(ng),
                  full(emat), full(bmask)],
        out_specs=[blk(D_SSD), blk(LANES), pl.BlockSpec((1, SUBLANES, L), lambda bi, ci: (bi, 0, ci))],
        out_shape=[jax.ShapeDtypeStruct((b, s, D_SSD), BF16),
                   jax.ShapeDtypeStruct((b, s, LANES), F32),
                   jax.ShapeDtypeStruct((b, SUBLANES, s), F32)],
        scratch_shapes=[pltpu.VMEM((L + SUBLANES, W_XBC), F32),
                        pltpu.VMEM((LANES, D_SSD), F32),
                        pltpu.VMEM((1, LANES), F32)],
        compiler_params=_cparams(("parallel", "arbitrary")),
        name="ssd_scan",
    )(xbc, z, sm, cw, alog, dexp, ng, emat, bmask)


def _fox_body(q_ref, k_ref, v_ref, crow_ref, o_ref, m_ref, l_ref, acc_ref):
    tq = q_ref.shape[1]
    tk = tq
    qi = pl.program_id(1)
    m_ref[...] = jnp.full_like(m_ref, NEG_INF)
    l_ref[...] = jnp.zeros_like(l_ref)
    acc_ref[...] = jnp.zeros_like(acc_ref)
    ri = lax.broadcasted_iota(I32, (tq, tk), 0)
    ci = lax.broadcasted_iota(I32, (tq, tk), 1)
    diag_mask = ci <= ri

    def tile(j, masked):
        k0 = pl.multiple_of(j * tk, tk)
        for pr in range(N_FOX_HEADS // 2):
            kslab = k_ref[0, pl.ds(k0, tk), LANES * pr:LANES * (pr + 1)]
            vslab = v_ref[0, pl.ds(k0, tk), LANES * pr:LANES * (pr + 1)]
            for hh in range(2):
                h = 2 * pr + hh
                q = q_ref[0, :, LANES * h:LANES * (h + 1)]
                s = _dot_nt(q, kslab)
                s = s - crow_ref[0, h:h + 1, pl.ds(k0, tk)]
                if masked:
                    s = jnp.where(diag_mask, s, NEG_INF)
                m_prev = m_ref[h]
                m_new = jnp.maximum(m_prev, jnp.max(s, axis=-1, keepdims=True))
                alpha = jnp.exp(m_prev - m_new)
                p = jnp.exp(s - m_new)
                l_ref[h] = alpha * l_ref[h] + jnp.sum(p, axis=-1, keepdims=True)
                acc_ref[h] = alpha * acc_ref[h] + _dot(p.astype(BF16), vslab)
                m_ref[h] = m_new

    def body(j, carry):
        tile(j, False)
        return carry

    lax.fori_loop(0, qi, body, 0)
    tile(qi, True)

    lane = lax.broadcasted_iota(I32, (tq, LANES), 1)
    lo = lane < HEAD_DIM
    outs = []
    for pr in range(N_FOX_HEADS // 2):
        tiny = jnp.finfo(F32).tiny
        o0 = acc_ref[2 * pr] / jnp.maximum(l_ref[2 * pr], tiny)
        o1 = acc_ref[2 * pr + 1] / jnp.maximum(l_ref[2 * pr + 1], tiny)
        outs.append(jnp.where(lo, o0, o1))
    o_ref[0] = jnp.concatenate(outs, axis=1).astype(BF16)


def _fox(qf, kf, vf, crow):
    b, s, _ = qf.shape
    tq = min(FOX_TQ, s)
    return pl.pallas_call(
        _fox_body,
        grid=(b, s // tq),
        in_specs=[pl.BlockSpec((1, tq, W_QF), lambda bi, qi: (bi, qi, 0)),
                  pl.BlockSpec((1, s, W_KF), lambda bi, qi: (bi, 0, 0)),
                  pl.BlockSpec((1, s, W_VF), lambda bi, qi: (bi, 0, 0)),
                  pl.BlockSpec((1, SUBLANES, s), lambda bi, qi: (bi, 0, 0))],
        out_specs=pl.BlockSpec((1, tq, D_FOX), lambda bi, qi: (bi, qi, 0)),
        out_shape=jax.ShapeDtypeStruct((b, s, D_FOX), BF16),
        scratch_shapes=[pltpu.VMEM((N_FOX_HEADS, tq, 1), F32),
                        pltpu.VMEM((N_FOX_HEADS, tq, 1), F32),
                        pltpu.VMEM((N_FOX_HEADS, tq, LANES), F32)],
        compiler_params=_cparams(("parallel", "arbitrary")),
        name="fox_attn",
    )(qf, kf, vf, crow)


def _cmp_body(gk_ref, gv_ref, wkt_ref, wkb_ref, wvt_ref, wvb_ref, pe_ref, gain_ref, o_ref):
    ncp = gk_ref.shape[1]
    gk = gk_ref[0]
    gv = gv_ref[0]
    top = _dot(gk, wkt_ref[...]) + _dot(gv, wvt_ref[...])
    bot = _dot(gk, wkb_ref[...]) + _dot(gv, wvb_ref[...])
    bot_next = pltpu.roll(bot, ncp - 1, 0)
    raw = top + bot_next + pe_ref[0:1, :]
    lane = lax.broadcasted_iota(I32, raw.shape, 1)
    lo = lane < HEAD_DIM
    msq = jnp.sum(jnp.where(lo, raw * raw, 0.0), axis=-1, keepdims=True) * (1.0 / HEAD_DIM)
    out = jnp.where(lo, raw * lax.rsqrt(msq + RMS_EPS) * gain_ref[...], raw)
    row = lax.broadcasted_iota(I32, raw.shape, 0)
    o_ref[0] = jnp.where(row < ncp - 1, out, 0.0).astype(BF16)


def _nsa_compress(kvn, pe_k, pe_v, w_ck, w_cv, k_norm):
    b, s, _ = kvn.shape
    half = NSA_CMP_STRIDE * HEAD_DIM
    ng = s // NSA_CMP_STRIDE
    gk = kvn[:, :, 0:HEAD_DIM].reshape(b, ng, half)
    gv = kvn[:, :, HEAD_DIM:2 * HEAD_DIM].reshape(b, ng, half)
    zc = jnp.zeros((half, HEAD_DIM), F32)
    wk = w_ck.astype(F32)
    wv = w_cv.astype(F32)
    wkt = jnp.concatenate([wk[:half], zc], axis=1).astype(BF16)
    wkb = jnp.concatenate([wk[half:], zc], axis=1).astype(BF16)
    wvt = jnp.concatenate([zc, wv[:half]], axis=1).astype(BF16)
    wvb = jnp.concatenate([zc, wv[half:]], axis=1).astype(BF16)
    pe_const = jnp.concatenate([jnp.dot(pe_k.reshape(1, -1).astype(F32), wk, precision=HIGHEST),
                                jnp.dot(pe_v.reshape(1, -1).astype(F32), wv, precision=HIGHEST)], axis=1)
    pe_const = jnp.concatenate([pe_const, jnp.zeros((SUBLANES - 1, LANES), F32)], axis=0)
    gain = jnp.concatenate([k_norm.astype(F32), jnp.ones((HEAD_DIM,), F32)]).reshape(1, LANES)
    blk = pl.BlockSpec((1, ng, half), lambda bi: (bi, 0, 0))
    full = lambda a: pl.BlockSpec(a.shape, lambda bi: (0, 0))
    return pl.pallas_call(
        _cmp_body,
        grid=(b,),
        in_specs=[blk, blk, full(wkt), full(wkb), full(wvt), full(wvb), full(pe_const), full(gain)],
        out_specs=pl.BlockSpec((1, ng, LANES), lambda bi: (bi, 0, 0)),
        out_shape=jax.ShapeDtypeStruct((b, ng, LANES), BF16),
        compiler_params=_cparams(("parallel",)),
        name="nsa_compress",
    )(gk, gv, wkt, wkb, wvt, wvb, pe_const, gain)


def _nsa_body(q_ref, kv_ref, kc_ref, sm_ref, ovt_ref, e_ref, o_ref,
              qs_ref, sc_ref, m_ref, l_ref, acc_ref, *, n_slc, top_n):
    tq = q_ref.shape[1]
    tk = e_ref.shape[2]
    ncp = kc_ref.shape[1]
    nh = N_NSA_HEADS
    rows = nh * tq
    qi = pl.program_id(1)
    t0 = qi * tq
    tiny = jnp.finfo(F32).tiny

    for h in range(nh):
        qs_ref[h * tq:(h + 1) * tq, :] = q_ref[0, :, LANES * h:LANES * (h + 1)]
    qs = qs_ref[...]

    kc = kc_ref[0]
    s_c = _dot_nt(qs, kc).reshape(nh, tq, ncp)
    t_q = t0 + lax.broadcasted_iota(I32, (1, tq, ncp), 1)
    c_i = lax.broadcasted_iota(I32, (1, tq, ncp), 2)
    valid_c = (c_i * NSA_CMP_STRIDE + (NSA_CMP_BLOCK - 1) <= t_q) & (c_i < ncp - 1)
    s_c = jnp.where(valid_c, s_c, NEG_INF)
    m_c = jnp.max(s_c, axis=-1, keepdims=True)
    e_c = jnp.exp(s_c - jnp.where(m_c == NEG_INF, 0.0, m_c))
    p_c = e_c / jnp.maximum(jnp.sum(e_c, axis=-1, keepdims=True), tiny)
    o_c = _dot(p_c.reshape(rows, ncp).astype(BF16), kc)

    p_sum = jnp.sum(p_c, axis=0)
    imp_t = _dot_nt(ovt_ref[...], p_sum, precision=HIGHEST)
    n_i = lax.broadcasted_iota(I32, (LANES, tq), 0)
    t_l = t0 + lax.broadcasted_iota(I32, (LANES, tq), 1)
    avail = n_i * NSA_SLC_BLOCK <= t_l
    forced = (n_i * NSA_SLC_BLOCK <= t_l) & (t_l < (n_i + 1) * NSA_SLC_BLOCK) | (n_i == 0)
    score = jnp.where(avail, jnp.where(forced, jnp.inf, imp_t), NEG_INF)
    sc_ref[...] = score
    ns_pad = ((n_slc + SUBLANES - 1) // SUBLANES) * SUBLANES
    sc = sc_ref[0:ns_pad, :]
    n_s = n_i[0:ns_pad, :]
    cnt = jnp.zeros((ns_pad, tq), F32)
    for mth in range(n_slc):
        r = sc_ref[mth:mth + 1, :]
        cnt = cnt + jnp.where(r > sc, 1.0, jnp.where((r == sc) & (n_s > mth), 1.0, 0.0))
    sel_t = jnp.where(avail[0:ns_pad, :] & (cnt < float(top_n)), 1.0, 0.0)
    sc_ref[0:ns_pad, :] = sel_t
    if ns_pad < LANES:
        sc_ref[ns_pad:LANES, :] = jnp.zeros((LANES - ns_pad, tq), F32)
    sel = sc_ref[...].T.astype(BF16)

    t_row = t0 + lax.broadcasted_iota(I32, (1, tq, tk), 1)
    k_col = lax.broadcasted_iota(I32, (1, tq, tk), 2)

    def attend(br, j, slab, mask):
        k0 = pl.multiple_of(j * tk, tk)
        kv = kv_ref[0, pl.ds(k0, tk), LANES * slab:LANES * (slab + 1)]
        s = _dot_nt(qs, kv).reshape(nh, tq, tk) + jnp.where(mask, 0.0, NEG_INF)
        m_prev = m_ref[br]
        m_new = jnp.maximum(m_prev, jnp.max(s, axis=-1, keepdims=True))
        m_safe = jnp.where(m_new == NEG_INF, 0.0, m_new)
        alpha = jnp.exp(m_prev - m_safe)
        p = jnp.exp(s - m_safe)
        l_ref[br] = alpha * l_ref[br] + jnp.sum(p, axis=-1, keepdims=True)
        pv = _dot(p.reshape(rows, tk).astype(BF16), kv).reshape(nh, tq, LANES)
        acc_ref[br] = alpha * acc_ref[br] + pv
        m_ref[br] = m_new

    m_ref[...] = jnp.full_like(m_ref, NEG_INF)
    l_ref[...] = jnp.zeros_like(l_ref)
    acc_ref[...] = jnp.zeros_like(acc_ref)

    def sel_body(j, carry):
        pos = j * tk + k_col
        blk_mask = _dot(sel, e_ref[j]) > 0.5
        attend(0, j, 1, blk_mask[None] & (pos <= t_row))
        return carry

    lax.fori_loop(0, (t0 + tq + tk - 1) // tk, sel_body, 0)

    def win_body(j, carry):
        pos = j * tk + k_col
        attend(1, j, 2, (pos <= t_row) & (pos > t_row - NSA_WINDOW))
        return carry

    w_lo = jnp.maximum(t0 - (NSA_WINDOW - 1), 0) // tk
    lax.fori_loop(w_lo, (t0 + tq + tk - 1) // tk, win_body, 0)

    sm = sm_ref[0]
    o_c3 = o_c.reshape(nh, tq, LANES)
    lane = lax.broadcasted_iota(I32, (tq, LANES), 1)
    lo = lane < HEAD_DIM
    heads = []
    for h in range(nh):
        g0 = sm[:, SM_G + 3 * h:SM_G + 3 * h + 1]
        g1 = sm[:, SM_G + 3 * h + 1:SM_G + 3 * h + 2]
        g2 = sm[:, SM_G + 3 * h + 2:SM_G + 3 * h + 3]
        o_s = acc_ref[0, h] / jnp.maximum(l_ref[0, h], tiny)
        o_w = acc_ref[1, h] / jnp.maximum(l_ref[1, h], tiny)
        heads.append(g0 * o_c3[h] + g1 * o_s + g2 * o_w)
    outs = []
    for pr in range(nh // 2):
        outs.append(jnp.where(lo, pltpu.roll(heads[2 * pr], HEAD_DIM, 1), heads[2 * pr + 1]))
    o_ref[0] = jnp.concatenate(outs, axis=1).astype(BF16)


def _nsa(qn, kvn, kc, sm):
    b, s, _ = qn.shape
    tq = min(NSA_TQ, s)
    tk = min(NSA_TK, s)
    n_cmp = (s - NSA_CMP_BLOCK) // NSA_CMP_STRIDE + 1
    n_slc = s // NSA_SLC_BLOCK
    ncp = kc.shape[1]
    assert n_cmp == ncp - 1 and n_slc <= LANES
    top_n = min(NSA_TOP_N, n_slc)
    c_start = jnp.arange(ncp) * NSA_CMP_STRIDE
    n_start = jnp.arange(LANES) * NSA_SLC_BLOCK
    ovt = ((c_start[None, :] < n_start[:, None] + NSA_SLC_BLOCK)
           & (c_start[None, :] + NSA_CMP_BLOCK > n_start[:, None])
           & (jnp.arange(ncp)[None, :] < n_cmp) & (jnp.arange(LANES)[:, None] < n_slc)).astype(F32)
    pos = jnp.arange(s).reshape(s // tk, 1, tk)
    emat = ((pos // NSA_SLC_BLOCK) == jnp.arange(LANES)[None, :, None]).astype(BF16)
    rows = N_NSA_HEADS * tq
    return pl.pallas_call(
        functools.partial(_nsa_body, n_slc=n_slc, top_n=top_n),
        grid=(b, s // tq),
        in_specs=[pl.BlockSpec((1, tq, W_QN), lambda bi, qi: (bi, qi, 0)),
                  pl.BlockSpec((1, s, W_KVN), lambda bi, qi: (bi, 0, 0)),
                  pl.BlockSpec((1, ncp, LANES), lambda bi, qi: (bi, 0, 0)),
                  pl.BlockSpec((1, tq, LANES), lambda bi, qi: (bi, qi, 0)),
                  pl.BlockSpec(ovt.shape, lambda bi, qi: (0, 0)),
                  pl.BlockSpec(emat.shape, lambda bi, qi: (0, 0, 0))],
        out_specs=pl.BlockSpec((1, tq, D_NSA), lambda bi, qi: (bi, qi, 0)),
        out_shape=jax.ShapeDtypeStruct((b, s, D_NSA), BF16),
        scratch_shapes=[pltpu.VMEM((rows, LANES), BF16),
                        pltpu.VMEM((LANES, tq), F32),
                        pltpu.VMEM((2, N_NSA_HEADS, tq, 1), F32),
                        pltpu.VMEM((2, N_NSA_HEADS, tq, 1), F32),
                        pltpu.VMEM((2, N_NSA_HEADS, tq, LANES), F32)],
        compiler_params=_cparams(("parallel", "arbitrary")),
        name="nsa_attn",
    )(qn, kvn, kc, sm, ovt, emat)


def _outproj_body(*refs, route):
    if route:
        (on_ref, os_ref, of_ref, x_ref, wn_ref, ws_ref, wf_ref, g_ref, wr_ref,
         xo_ref, h_ref, rt_ref) = refs
    else:
        on_ref, os_ref, of_ref, x_ref, wn_ref, ws_ref, wf_ref, g_ref, xo_ref, h_ref = refs
    x = (x_ref[...] + _dot(on_ref[...], wn_ref[...]) + _dot(os_ref[...], ws_ref[...])
         + _dot(of_ref[...], wf_ref[...]))
    xo_ref[...] = x
    ms = jnp.mean(x * x, axis=-1, keepdims=True)
    h = x * lax.rsqrt(ms + RMS_EPS) * g_ref[...]
    h_ref[...] = h.astype(h_ref.dtype)
    if route:
        logits = _dot(h, wr_ref[...], precision=HIGHEST)
        lane = lax.broadcasted_iota(I32, logits.shape, 1)
        lf = lane.astype(F32)
        lg = jnp.where(lane < N_EXPERTS, logits, NEG_INF)
        m1 = jnp.max(lg, axis=-1, keepdims=True)
        i1 = jnp.min(jnp.where(lg == m1, lf, float(LANES)), axis=-1, keepdims=True)
        lg2 = jnp.where(lf == i1, NEG_INF, lg)
        m2 = jnp.max(lg2, axis=-1, keepdims=True)
        i2 = jnp.min(jnp.where(lg2 == m2, lf, float(LANES)), axis=-1, keepdims=True)
        e21 = jnp.exp(m2 - m1)
        g1 = 1.0 / (1.0 + e21)
        g2 = e21 / (1.0 + e21)
        rt_ref[...] = jnp.where(lane == 0, i1, jnp.where(lane == 1, i2,
                                jnp.where(lane == 2, g1, jnp.where(lane == 3, g2, 0.0))))


def _outproj(o_nsa, o_ssd, o_fox, x2, w_out, g, w_router=None):
    n, d = x2.shape
    tm = min(TM_PROJ, n)
    route = w_router is not None
    wn = w_out[0:D_NSA].astype(BF16)
    ws = w_out[D_NSA:D_NSA + D_SSD].astype(BF16)
    wf = w_out[D_NSA + D_SSD:].astype(BF16)
    gg = g.astype(F32).reshape(1, d)
    row = lambda wd: pl.BlockSpec((tm, wd), lambda i: (i, 0))
    full = lambda a: pl.BlockSpec(a.shape, lambda i: (0, 0))
    ins = [o_nsa, o_ssd, o_fox, x2, wn, ws, wf, gg]
    in_specs = [row(D_NSA), row(D_SSD), row(D_FOX), row(d), full(wn), full(ws), full(wf), full(gg)]
    out_specs = [row(d), row(d)]
    out_shape = [jax.ShapeDtypeStruct((n, d), F32),
                 jax.ShapeDtypeStruct((n, d), F32 if route else BF16)]
    if route:
        wr = jnp.concatenate([w_router.astype(F32), jnp.zeros((d, LANES - N_EXPERTS), F32)], axis=1)
        ins.append(wr)
        in_specs.append(full(wr))
        out_specs.append(row(LANES))
        out_shape.append(jax.ShapeDtypeStruct((n, LANES), F32))
    return pl.pallas_call(
        functools.partial(_outproj_body, route=route),
        grid=(n // tm,),
        in_specs=in_specs, out_specs=out_specs, out_shape=out_shape,
        compiler_params=_cparams(("parallel",)),
        name="outproj_route" if route else "outproj",
    )(*ins)


def _swiglu_acc(h, w1_ref, w3_ref, w2_ref, acc_ref, f):
    @pl.when(f == 0)
    def _():
        acc_ref[...] = jnp.zeros_like(acc_ref)

    a = _dot(h, w1_ref[...])
    b = _dot(h, w3_ref[...])
    mid = (a * (1.0 / (1.0 + jnp.exp(-a))) * b).astype(BF16)
    acc_ref[...] += _dot(mid, w2_ref[...])


def _ffn_body(h_ref, x_ref, w1_ref, w3_ref, w2_ref, o_ref, acc_ref):
    f = pl.program_id(1)
    _swiglu_acc(h_ref[...], w1_ref, w3_ref, w2_ref, acc_ref, f)

    @pl.when(f == pl.num_programs(1) - 1)
    def _():
        o_ref[...] = x_ref[...] + acc_ref[...]


def _ffn_dense(h, x2, w1, w3, w2):
    n, d = x2.shape
    ff = w1.shape[1]
    tm = min(FFN_TM, n)
    tf = FFN_TF
    return pl.pallas_call(
        _ffn_body,
        grid=(n // tm, ff // tf),
        in_specs=[pl.BlockSpec((tm, d), lambda i, f: (i, 0)),
                  pl.BlockSpec((tm, d), lambda i, f: (i, 0)),
                  pl.BlockSpec((d, tf), lambda i, f: (0, f)),
                  pl.BlockSpec((d, tf), lambda i, f: (0, f)),
                  pl.BlockSpec((tf, d), lambda i, f: (f, 0))],
        out_specs=pl.BlockSpec((tm, d), lambda i, f: (i, 0)),
        out_shape=jax.ShapeDtypeStruct((n, d), F32),
        scratch_shapes=[pltpu.VMEM((tm, d), F32)],
        compiler_params=_cparams(("parallel", "arbitrary")),
        name="ffn_dense",
    )(h, x2, w1.astype(BF16), w3.astype(BF16), w2.astype(BF16))


def _gffn_body(be_ref, xs_ref, w1_ref, w3_ref, w2_ref, o_ref, acc_ref):
    f = pl.program_id(1)
    _swiglu_acc(xs_ref[...].astype(BF16), w1_ref.at[0], w3_ref.at[0], w2_ref.at[0], acc_ref, f)

    @pl.when(f == pl.num_programs(1) - 1)
    def _():
        o_ref[...] = acc_ref[...]


def _ffn_grouped(xs, blk_e, w1, w3, w2):
    p, d = xs.shape
    ff = w1.shape[2]
    tm = MOE_TM
    tf = FFN_TF
    grid_spec = pltpu.PrefetchScalarGridSpec(
        num_scalar_prefetch=1,
        grid=(p // tm, ff // tf),
        in_specs=[pl.BlockSpec((tm, d), lambda i, f, be: (i, 0)),
                  pl.BlockSpec((1, d, tf), lambda i, f, be: (be[i], 0, f)),
                  pl.BlockSpec((1, d, tf), lambda i, f, be: (be[i], 0, f)),
                  pl.BlockSpec((1, tf, d), lambda i, f, be: (be[i], f, 0))],
        out_specs=pl.BlockSpec((tm, d), lambda i, f, be: (i, 0)),
        scratch_shapes=[pltpu.VMEM((tm, d), F32)])
    return pl.pallas_call(
        _gffn_body,
        grid_spec=grid_spec,
        out_shape=jax.ShapeDtypeStruct((p, d), F32),
        compiler_params=_cparams(("parallel", "arbitrary")),
        name="ffn_grouped",
    )(blk_e, xs, w1.astype(BF16), w3.astype(BF16), w2.astype(BF16))


def _row_copy(src, dst, sem):
    return pltpu.make_async_copy(src, dst, sem)


def _dispatch_body(tok_ref, h_hbm, o_ref, buf_ref, sem):
    tt = o_ref.shape[0]

    def issue(i, carry):
        for u in range(2):
            r = 2 * i + u
            t = tok_ref[0, 0, r]
            _row_copy(h_hbm.at[pl.ds(t, 1)], buf_ref.at[pl.ds(r, 1)], sem).start(priority=u)
        return carry

    lax.fori_loop(0, tt // 2, issue, 0, unroll=4)

    def drain(r, carry):
        _row_copy(h_hbm.at[pl.ds(0, 1)], buf_ref.at[pl.ds(0, 1)], sem).wait()
        return carry

    lax.fori_loop(0, tt, drain, 0)
    o_ref[...] = buf_ref[...].astype(o_ref.dtype)


def _dispatch(h, tok3):
    n, d = h.shape
    nt, _, tt = tok3.shape
    return pl.pallas_call(
        _dispatch_body,
        grid=(nt,),
        in_specs=[pl.BlockSpec((1, 1, tt), lambda i: (i, 0, 0), memory_space=pltpu.SMEM),
                  pl.BlockSpec(memory_space=pl.ANY)],
        out_specs=pl.BlockSpec((tt, d), lambda i: (i, 0)),
        out_shape=jax.ShapeDtypeStruct((nt * tt, d), BF16),
        scratch_shapes=[pltpu.VMEM((tt, d), h.dtype), pltpu.SemaphoreType.DMA(())],
        compiler_params=_cparams(("arbitrary",)),
        name="moe_dispatch",
    )(tok3, h)


def _combine_body(dest_ref, x_ref, rt_ref, y_hbm, o_ref, buf_ref, sem):
    tt = x_ref.shape[0]

    def issue(r, carry):
        for k in range(TOP_K):
            d = dest_ref[0, k, r]
            _row_copy(y_hbm.at[pl.ds(d, 1)], buf_ref.at[k, pl.ds(r, 1)], sem).start(priority=k)
        return carry

    lax.fori_loop(0, tt, issue, 0, unroll=8)

    def drain(r, carry):
        for k in range(TOP_K):
            _row_copy(y_hbm.at[pl.ds(0, 1)], buf_ref.at[k, pl.ds(0, 1)], sem).wait()
        return carry

    lax.fori_loop(0, tt, drain, 0)
    rt = rt_ref[...]
    o_ref[...] = x_ref[...] + rt[:, 2:3] * buf_ref[0] + rt[:, 3:4] * buf_ref[1]


def _combine(x2, route, dest3, y):
    n, d = x2.shape
    tt = dest3.shape[2]
    return pl.pallas_call(
        _combine_body,
        grid=(n // tt,),
        in_specs=[pl.BlockSpec((1, TOP_K, tt), lambda i: (i, 0, 0), memory_space=pltpu.SMEM),
                  pl.BlockSpec((tt, d), lambda i: (i, 0)),
                  pl.BlockSpec((tt, LANES), lambda i: (i, 0)),
                  pl.BlockSpec(memory_space=pl.ANY)],
        out_specs=pl.BlockSpec((tt, d), lambda i: (i, 0)),
        out_shape=jax.ShapeDtypeStruct((n, d), F32),
        scratch_shapes=[pltpu.VMEM((TOP_K, tt, d), F32), pltpu.SemaphoreType.DMA(())],
        compiler_params=_cparams(("arbitrary",)),
        name="moe_combine",
    )(dest3, x2, route, y)


def _moe(h, x2, route, w1, w3, w2):
    n, d = x2.shape
    tm = MOE_TM
    tt = min(MOE_TT, n)
    n_rows = n * TOP_K
    flat_e = route[:, 0:TOP_K].astype(I32).reshape(-1)
    onehot = (flat_e[:, None] == jnp.arange(N_EXPERTS)[None, :]).astype(I32)
    csum = jnp.cumsum(onehot, axis=0)
    rank = jnp.sum(csum * onehot, axis=1) - 1
    counts = csum[-1]
    padded = ((counts + tm - 1) // tm) * tm
    pend = jnp.cumsum(padded)
    pstart = pend - padded
    dest = jnp.sum(onehot * pstart[None, :], axis=1) + rank
    n_blk = -(-n_rows // tm) + N_EXPERTS
    blk_e = jnp.minimum(jnp.searchsorted(pend, jnp.arange(n_blk) * tm, side="right"),
                        N_EXPERTS - 1).astype(I32)
    dest3 = dest.reshape(n // tt, tt, TOP_K).transpose(0, 2, 1).astype(I32)
    pad_i = jnp.arange(tm)[None, :]
    pad_key = jnp.where(pad_i < (padded - counts)[:, None], jnp.arange(N_EXPERTS)[:, None], N_EXPERTS)
    keys = jnp.concatenate([flat_e, pad_key.reshape(-1).astype(I32)])
    toks = jnp.concatenate([jnp.arange(n_rows, dtype=I32) // TOP_K, jnp.zeros((N_EXPERTS * tm,), I32)])
    _, tok_sorted = lax.sort((keys, toks), num_keys=1, is_stable=True)
    tok3 = tok_sorted[:n_blk * tm].reshape(-1, 1, tt)
    xs = _dispatch(h, tok3)
    y = _ffn_grouped(xs, blk_e, w1, w3, w2)
    return _combine(x2, route, dest3, y)


def kernel(x, attn_norm, w_in, nsa_q_norm, nsa_k_norm, nsa_cmp_pe_k, nsa_cmp_pe_v, nsa_cmp_w_k,
           nsa_cmp_w_v, ssd_conv_w, ssd_conv_b, ssd_dt_bias, ssd_a_log, ssd_d, ssd_norm, fox_q_norm,
           fox_k_norm, fox_f_bias, w_out, ffn_norm, ffn_w1, ffn_w3, ffn_w2, moe_router, moe_w1,
           moe_w3, moe_w2):
    b, s, d = x.shape
    n = b * s
    depth = w_in.shape[0]
    x2 = x.reshape(n, d).astype(F32)
    for layer in range(depth):
        w, gains, sbias = _pack_in_weights(w_in[layer], nsa_q_norm[layer], nsa_k_norm[layer],
                                           fox_q_norm[layer], fox_k_norm[layer],
                                           ssd_dt_bias[layer], fox_f_bias[layer])
        g = attn_norm[layer].astype(F32).reshape(1, d)
        qn, kvn, z, xbc, qf, kf, vf, sm = _inproj(x2, g, w, gains, sbias)
        r3 = lambda a: a.reshape(b, s, a.shape[-1])
        kvn3, sm3 = r3(kvn), r3(sm)
        kc = _nsa_compress(kvn3, nsa_cmp_pe_k[layer], nsa_cmp_pe_v[layer],
                           nsa_cmp_w_k[layer], nsa_cmp_w_v[layer], nsa_k_norm[layer])
        o_nsa = _nsa(r3(qn), kvn3, kc, sm3)
        o_ssd, ccol, crow = _ssd(r3(xbc), r3(z), sm3, ssd_conv_w[layer], ssd_conv_b[layer],
                                 ssd_a_log[layer], ssd_d[layer], ssd_norm[layer])
        o_fox = _fox(r3(qf), r3(kf), r3(vf), crow)
        idx = layer // 2
        is_moe = layer % 2 == 1
        res = _outproj(o_nsa.reshape(n, D_NSA), o_ssd.reshape(n, D_SSD), o_fox.reshape(n, D_FOX),
                       x2, w_out[layer], ffn_norm[layer], moe_router[idx] if is_moe else None)
        if is_moe:
            x2, h, route = res
            x2 = _moe(h, x2, route, moe_w1[idx], moe_w3[idx], moe_w2[idx])
        else:
            x2, h = res
            x2 = _ffn_dense(h, x2, ffn_w1[idx], ffn_w3[idx], ffn_w2[idx])
    return x2.reshape(b, s, d).astype(x.dtype)
```
